```python
import math
import jax, jax.numpy as jnp
from jax import lax
import numpy as np

D_MODEL = 1024
BATCH = 8
SEQ = 4096
DEPTH = 4

MIX_WIDTH = D_MODEL
HEAD_DIM = 64
SGU_WIDTH = MIX_WIDTH // 4
SGU_GROUPS = SGU_WIDTH // HEAD_DIM
SGU_CHUNK = 128
SSM_INNER = MIX_WIDTH // 2
SSM_HEAD_DIM = 64
SSM_HEADS = SSM_INNER // SSM_HEAD_DIM
SSM_GROUPS = 2
SSM_STATE = 128
SSM_CONV = 4
SSM_CHUNK = 128
SSM_CONV_CH = SSM_INNER + 2 * SSM_GROUPS * SSM_STATE
ATTN_WIDTH = MIX_WIDTH - SGU_WIDTH - SSM_INNER
ATTN_HEADS = ATTN_WIDTH // HEAD_DIM
DILATED_PATTERNS = ((128, 1), (512, 4), (2048, 16))
ATTN_BLOCK = 128
REL_BUCKETS = 32
REL_MAX_DIST = 2048
MEM_LEN = 256
XATTN_HEADS = 4
XATTN_HEAD_DIM = D_MODEL // XATTN_HEADS
D_FF = 2816
EPS = 1e-6

OFF_SGU = 0
OFF_Z = OFF_SGU + 2 * SGU_WIDTH
OFF_XBC = OFF_Z + SSM_INNER
OFF_DT = OFF_XBC + SSM_CONV_CH
OFF_QKV = OFF_DT + SSM_HEADS
N_IN = OFF_QKV + 3 * ATTN_WIDTH

kernel_name = "hybrid_sgu_ssd_dilated_macaron_trunk"


def rms_norm(x, g):
    xf = x.astype(jnp.float32)
    y = xf * lax.rsqrt(jnp.mean(xf * xf, axis=-1, keepdims=True) + EPS)
    return (y * g.astype(jnp.float32)).astype(x.dtype)


def swiglu(x, wi, wo):
    g, u = jnp.split(x @ wi, 2, axis=-1)
    return (jax.nn.silu(g) * u) @ wo


def t5_bucket(dist):
    max_exact = REL_BUCKETS // 2
    d = jnp.maximum(dist, 1).astype(jnp.float32)
    large = max_exact + (jnp.log(d / max_exact) / math.log(REL_MAX_DIST / max_exact)
                         * (REL_BUCKETS - max_exact)).astype(jnp.int32)
    large = jnp.minimum(large, REL_BUCKETS - 1)
    return jnp.where(dist < max_exact, dist, large)


def spatial_gating(uv, ln_g, w_s, b_s):
    B_, S_, _ = uv.shape
    u, v = jnp.split(jax.nn.gelu(uv).astype(jnp.float32), 2, axis=-1)
    mu = jnp.mean(v, axis=-1, keepdims=True)
    var = jnp.mean(jnp.square(v - mu), axis=-1, keepdims=True)
    vn = (v - mu) * lax.rsqrt(var + EPS) * ln_g.astype(jnp.float32)
    nc = S_ // SGU_CHUNK
    vn = vn.reshape(B_, nc, SGU_CHUNK, SGU_GROUPS, HEAD_DIM)
    causal = jnp.tril(jnp.ones((SGU_CHUNK, SGU_CHUNK), bool))
    w = jnp.where(causal[None], w_s.astype(jnp.float32), 0.0)
    s = jnp.einsum('gts,bcsgd->bctgd', w, vn) + b_s.astype(jnp.float32).T[None, None, :, :, None]
    return (u * s.reshape(B_, S_, SGU_WIDTH)).astype(uv.dtype)


def ssd_scan(xs, dt, A, Bm, Cm):
    B_, S_, H_, P_ = xs.shape
    G_, N_ = Bm.shape[2], Bm.shape[3]
    E_ = H_ // G_
    T = SSM_CHUNK
    nc = S_ // T
    x = (xs * dt[..., None]).reshape(B_, nc, T, G_, E_, P_)
    a = (dt * A).reshape(B_, nc, T, G_, E_)
    Bc = Bm.reshape(B_, nc, T, G_, N_)
    Cc = Cm.reshape(B_, nc, T, G_, N_)
    acs = jnp.cumsum(a, axis=2)
    causal = jnp.tril(jnp.ones((T, T), bool))[:, :, None, None]
    seg = acs[:, :, :, None] - acs[:, :, None, :]
    decay_ls = jnp.exp(jnp.where(causal, seg, -jnp.inf))
    cb = jnp.einsum('bclgn,bcsgn->bclsg', Cc, Bc)
    y_diag = jnp.einsum('bclsge,bcsgep->bclgep', cb[..., None] * decay_ls, x)
    decay_to_end = jnp.exp(acs[:, :, -1:] - acs)
    states = jnp.einsum('bcsgn,bcsge,bcsgep->bcgepn', Bc, decay_to_end, x)
    chunk_decay = jnp.exp(acs[:, :, -1])

    def step(h, inp):
        dec, st = inp
        return h * dec[..., None, None] + st, h

    h0 = jnp.zeros((B_, G_, E_, P_, N_), jnp.float32)
    _, h_in = lax.scan(step, h0, (jnp.moveaxis(chunk_decay, 1, 0), jnp.moveaxis(states, 1, 0)))
    h_in = jnp.moveaxis(h_in, 0, 1)
    y_off = jnp.einsum('bclgn,bcgepn->bclgep', Cc, h_in) * jnp.exp(acs)[..., None]
    return (y_diag + y_off).reshape(B_, S_, H_, P_)


def ssd_mixer(z, xbc, dt_raw, conv_w, conv_b, dt_bias, a_log, d_skip, norm_g):
    B_, S_, _ = xbc.shape
    xbc = lax.conv_general_dilated(xbc, conv_w.astype(xbc.dtype)[:, None, :], window_strides=(1,),
                                   padding=[(SSM_CONV - 1, 0)], dimension_numbers=('NWC', 'WIO', 'NWC'),
                                   feature_group_count=SSM_CONV_CH) + conv_b
    xbc = jax.nn.silu(xbc).astype(jnp.float32)
    xs = xbc[..., :SSM_INNER].reshape(B_, S_, SSM_HEADS, SSM_HEAD_DIM)
    Bm = xbc[..., SSM_INNER:SSM_INNER + SSM_GROUPS * SSM_STATE].reshape(B_, S_, SSM_GROUPS, SSM_STATE)
    Cm = xbc[..., SSM_INNER + SSM_GROUPS * SSM_STATE:].reshape(B_, S_, SSM_GROUPS, SSM_STATE)
    dt = jax.nn.softplus(dt_raw.astype(jnp.float32) + dt_bias.astype(jnp.float32))
    A = -jnp.exp(a_log.astype(jnp.float32))
    y = ssd_scan(xs, dt, A, Bm, Cm) + d_skip.astype(jnp.float32)[:, None] * xs
    y = y.reshape(B_, S_, SSM_INNER) * jax.nn.silu(z.astype(jnp.float32))
    yg = y.reshape(B_, S_, SSM_GROUPS, SSM_INNER // SSM_GROUPS)
    yg = yg * lax.rsqrt(jnp.mean(yg * yg, axis=-1, keepdims=True) + EPS)
    return (yg.reshape(B_, S_, SSM_INNER) * norm_g.astype(jnp.float32)).astype(z.dtype)


def dilated_branch(q, k, v, rel_bias, window, dil):
    B_, S_, H_, E_ = q.shape
    T = ATTN_BLOCK
    span = window // dil
    L = S_ // dil
    nb = -(-L // T)
    Lp = nb * T

    def residues(t, front):
        t = t.reshape(B_, L, dil, H_, E_)
        return jnp.pad(t, ((0, 0), (front, Lp - L), (0, 0), (0, 0), (0, 0)))

    qb = residues(q, 0).reshape(B_, nb, T, dil, H_, E_)
    kb = residues(k, T).reshape(B_, nb + 1, T, dil, H_, E_)
    vb = residues(v, T).reshape(B_, nb + 1, T, dil, H_, E_)
    kw = jnp.concatenate([kb[:, :-1], kb[:, 1:]], axis=2)
    vw = jnp.concatenate([vb[:, :-1], vb[:, 1:]], axis=2)
    dist = jnp.arange(T)[:, None] + T - jnp.arange(2 * T)[None, :]
    band = (dist >= 0) & (dist <= span)
    key_idx = jnp.arange(nb)[:, None] * T - T + jnp.arange(2 * T)[None, :]
    mask = band[None] & (key_idx >= 0)[:, None, :]
    bias = jnp.transpose(rel_bias[t5_bucket(jnp.maximum(dist, 0) * dil)], (2, 0, 1))
    logits = jnp.einsum('bnirhe,bnjrhe->bnrhij', qb, kw).astype(jnp.float32) * (E_ ** -0.5)
    logits = logits + bias.astype(jnp.float32)
    logits = jnp.where(mask[None, :, None, None], logits, -jnp.inf)
    m = jnp.max(logits, axis=-1)
    p = jnp.exp(logits - m[..., None])
    s = jnp.sum(p, axis=-1)
    o = jnp.einsum('bnrhij,bnjrhe->bnirhe', p, vw.astype(jnp.float32))
    o = o / jnp.transpose(s, (0, 1, 4, 2, 3))[..., None]
    lse = jnp.transpose(m + jnp.log(s), (0, 1, 4, 2, 3))
    o = o.reshape(B_, Lp, dil, H_, E_)[:, :L].reshape(B_, S_, H_, E_)
    lse = lse.reshape(B_, Lp, dil, H_)[:, :L].reshape(B_, S_, H_)
    return o, lse


def dilated_attention(q, k, v, rel_bias):
    outs, lses = [], []
    for window, dil in DILATED_PATTERNS:
        o, lse = dilated_branch(q, k, v, rel_bias, window, dil)
        outs.append(o)
        lses.append(lse)
    w = jax.nn.softmax(jnp.stack(lses), axis=0)
    return jnp.einsum('kbsh,kbshe->bshe', w, jnp.stack(outs)).astype(q.dtype)


def token_mixer(hn, w_in, w_out, sgu_ln_g, sgu_w, sgu_b, conv_w, conv_b, dt_bias, a_log, d_skip,
                ssm_norm_g, rel_bias):
    B_, S_, _ = hn.shape
    proj = hn @ w_in
    a_out = spatial_gating(proj[..., OFF_SGU:OFF_Z], sgu_ln_g, sgu_w, sgu_b)
    b_out = ssd_mixer(proj[..., OFF_Z:OFF_XBC], proj[..., OFF_XBC:OFF_DT], proj[..., OFF_DT:OFF_QKV],
                      conv_w, conv_b, dt_bias, a_log, d_skip, ssm_norm_g)
    q, k, v = jnp.split(proj[..., OFF_QKV:].reshape(B_, S_, 3, ATTN_HEADS, HEAD_DIM), 3, axis=2)
    c_out = dilated_attention(q[:, :, 0], k[:, :, 0], v[:, :, 0], rel_bias).reshape(B_, S_, ATTN_WIDTH)
    mixed = jnp.concatenate([a_out.astype(hn.dtype), b_out.astype(hn.dtype), c_out], axis=-1)
    return mixed @ w_out


def memory_cross_attention(hn, mem_n, wq, wkv, wo):
    B_, S_, _ = hn.shape
    M_ = mem_n.shape[1]
    q = (hn @ wq).reshape(B_, S_, XATTN_HEADS, XATTN_HEAD_DIM)
    k, v = jnp.split((mem_n @ wkv).reshape(B_, M_, 2, XATTN_HEADS, XATTN_HEAD_DIM), 2, axis=2)
    logits = jnp.einsum('bshe,bmhe->bhsm', q, k[:, :, 0]).astype(jnp.float32) * (XATTN_HEAD_DIM ** -0.5)
    p = jax.nn.softmax(logits, axis=-1)
    o = jnp.einsum('bhsm,bmhe->bshe', p, v[:, :, 0].astype(jnp.float32)).astype(hn.dtype)
    return o.reshape(B_, S_, D_MODEL) @ wo


def setup_inputs(seed: int = 0) -> dict:
    key = jax.random.key(seed)
    ks = jax.random.split(key, 24)
    f32 = jnp.float32

    def nrm(k, shape, scale):
        return jax.random.normal(k, shape, f32) * scale

    dt0 = jnp.exp(jax.random.uniform(ks[13], (DEPTH, SSM_HEADS), f32, math.log(1e-3), math.log(1e-1)))
    return {
        'x': nrm(ks[0], (BATCH, SEQ, D_MODEL), 1.0),
        'mem': nrm(ks[1], (BATCH, MEM_LEN, D_MODEL), 1.0),
        'norm_pre': 1.0 + nrm(ks[2], (DEPTH, 4, D_MODEL), 0.05),
        'norm_post': 1.0 + nrm(ks[3], (DEPTH, 4, D_MODEL), 0.05),
        'ffn_wi': nrm(ks[4], (DEPTH, 2, D_MODEL, 2 * D_FF), D_MODEL ** -0.5),
        'ffn_wo': nrm(ks[5], (DEPTH, 2, D_FF, D_MODEL), D_FF ** -0.5),
        'mix_w_in': nrm(ks[6], (DEPTH, D_MODEL, N_IN), D_MODEL ** -0.5),
        'mix_w_out': nrm(ks[7], (DEPTH, MIX_WIDTH, D_MODEL), MIX_WIDTH ** -0.5),
        'sgu_ln_g': 1.0 + nrm(ks[8], (DEPTH, SGU_WIDTH), 0.05),
        'sgu_w': nrm(ks[9], (DEPTH, SGU_GROUPS, SGU_CHUNK, SGU_CHUNK), SGU_CHUNK ** -0.5),
        'sgu_b': 1.0 + nrm(ks[10], (DEPTH, SGU_GROUPS, SGU_CHUNK), 0.1),
        'ssm_conv_w': nrm(ks[11], (DEPTH, SSM_CONV, SSM_CONV_CH), SSM_CONV ** -0.5),
        'ssm_conv_b': nrm(ks[12], (DEPTH, SSM_CONV_CH), 0.02),
        'ssm_dt_bias': dt0 + jnp.log(-jnp.expm1(-dt0)),
        'ssm_a_log': jnp.log(jax.random.uniform(ks[14], (DEPTH, SSM_HEADS), f32, 1.0, 16.0)),
        'ssm_d': 1.0 + nrm(ks[15], (DEPTH, SSM_HEADS), 0.1),
        'ssm_norm_g': 1.0 + nrm(ks[16], (DEPTH, SSM_INNER), 0.05),
        'rel_bias': nrm(ks[17], (REL_BUCKETS, ATTN_HEADS), 0.5),
        'mem_norm_g': 1.0 + nrm(ks[18], (DEPTH, D_MODEL), 0.05),
        'xattn_wq': nrm(ks[19], (DEPTH, D_MODEL, D_MODEL), D_MODEL ** -0.5),
        'xattn_wkv': nrm(ks[20], (DEPTH, D_MODEL, 2 * D_MODEL), D_MODEL ** -0.5),
        'xattn_wo': nrm(ks[21], (DEPTH, D_MODEL, D_MODEL), D_MODEL ** -0.5),
    }


def reference(x, mem, norm_pre, norm_post, ffn_wi, ffn_wo, mix_w_in, mix_w_out, sgu_ln_g, sgu_w, sgu_b,
              ssm_conv_w, ssm_conv_b, ssm_dt_bias, ssm_a_log, ssm_d, ssm_norm_g, rel_bias, mem_norm_g,
              xattn_wq, xattn_wkv, xattn_wo):
    h = x
    for l in range(DEPTH):
        hn = rms_norm(h, norm_pre[l, 0])
        h = h + 0.5 * rms_norm(swiglu(hn, ffn_wi[l, 0], ffn_wo[l, 0]), norm_post[l, 0])
        hn = rms_norm(h, norm_pre[l, 1])
        mixed = token_mixer(hn, mix_w_in[l], mix_w_out[l], sgu_ln_g[l], sgu_w[l], sgu_b[l], ssm_conv_w[l],
                            ssm_conv_b[l], ssm_dt_bias[l], ssm_a_log[l], ssm_d[l], ssm_norm_g[l], rel_bias)
        h = h + rms_norm(mixed, norm_post[l, 1])
        hn = rms_norm(h, norm_pre[l, 2])
        mem_n = rms_norm(mem, mem_norm_g[l])
        h = h + rms_norm(memory_cross_attention(hn, mem_n, xattn_wq[l], xattn_wkv[l], xattn_wo[l]), norm_post[l, 2])
        hn = rms_norm(h, norm_pre[l, 3])
        h = h + 0.5 * rms_norm(swiglu(hn, ffn_wi[l, 1], ffn_wo[l, 1]), norm_post[l, 3])
    return h
```

```python
import functools
import math

import jax
import jax.numpy as jnp
from jax import lax
from jax.experimental import pallas as pl
from jax.experimental.pallas import tpu as pltpu

F32 = jnp.float32
BF16 = jnp.bfloat16
EPS = 1e-6

HEAD_DIM = 64
CHUNK = 128
SGU_WIDTH = 256
SGU_GROUPS = SGU_WIDTH // HEAD_DIM
SSM_INNER = 512
SSM_HEADS = SSM_INNER // HEAD_DIM
SSM_GROUPS = 2
SSM_STATE = 128
SSM_CONV = 4
SSM_CONV_CH = SSM_INNER + 2 * SSM_GROUPS * SSM_STATE
GROUP_W = SSM_INNER // SSM_GROUPS
ATTN_WIDTH = 256
ATTN_HEADS = ATTN_WIDTH // HEAD_DIM
DILATED_PATTERNS = ((128, 1), (512, 4), (2048, 16))
REL_BUCKETS = 32
REL_MAX_DIST = 2048
XATTN_HEADS = 4

OFF_Z = 2 * SGU_WIDTH
OFF_XBC = OFF_Z + SSM_INNER
OFF_DT = OFF_XBC + SSM_CONV_CH
OFF_QKV = OFF_DT + SSM_HEADS
W_MAIN = OFF_DT + 3 * ATTN_WIDTH

ROW_TILE = 512
FF_TILE = 256
SSD_TILE = 512
ATTN_TILE = 1024
CONV_PAD = 8
VMEM_LIMIT = 56 * 1024 * 1024


def _params(n_axes):
    return pltpu.CompilerParams(dimension_semantics=("arbitrary",) * n_axes,
                                vmem_limit_bytes=VMEM_LIMIT)


def _rms(x, g):
    return x * lax.rsqrt(jnp.mean(x * x, axis=-1, keepdims=True) + EPS) * g


def _silu(x):
    return x * jax.nn.sigmoid(x)


def _dot(a, b):
    return jnp.dot(a, b, preferred_element_type=F32)


def _dot_nt(a, b):
    return lax.dot_general(a, b, (((1,), (1,)), ((), ())), preferred_element_type=F32)


def _const_spec(shape, index):
    return pl.BlockSpec(shape, lambda *_: index)


def _ffn_body(h_ref, gpre_ref, wi_ref, wo_ref, gpost_ref, o_ref, acc_ref, *, d_ff):
    h = h_ref[...]
    hn = _rms(h, gpre_ref[...]).astype(BF16)
    for c in range(d_ff // FF_TILE):
        lo = c * FF_TILE
        g = _dot(hn, wi_ref[:, lo:lo + FF_TILE])
        u = _dot(hn, wi_ref[:, d_ff + lo:d_ff + lo + FF_TILE])
        y = _dot((_silu(g) * u).astype(BF16), wo_ref[lo:lo + FF_TILE, :])
        if c == 0:
            acc_ref[...] = y
        else:
            acc_ref[...] += y
    o_ref[...] = h + 0.5 * _rms(acc_ref[...], gpost_ref[...])


def _ffn(h, gpre, wi, wo, gpost, layer, which):
    rows, d = h.shape
    d_ff = wo.shape[2]
    return pl.pallas_call(
        functools.partial(_ffn_body, d_ff=d_ff),
        out_shape=jax.ShapeDtypeStruct((rows, d), F32),
        grid=(rows // ROW_TILE,),
        in_specs=[
            pl.BlockSpec((ROW_TILE, d), lambda i: (i, 0)),
            _const_spec((None, None, 1, d), (layer, 3 * which, 0, 0)),
            _const_spec((None, None, d, 2 * d_ff), (layer, which, 0, 0)),
            _const_spec((None, None, d_ff, d), (layer, which, 0, 0)),
            _const_spec((None, None, 1, d), (layer, 3 * which, 0, 0)),
        ],
        out_specs=pl.BlockSpec((ROW_TILE, d), lambda i: (i, 0)),
        scratch_shapes=[pltpu.VMEM((ROW_TILE, d), F32)],
        compiler_params=_params(1),
        name="ffn",
    )(h, gpre, wi, wo, gpost)


def _inproj_body(h_ref, gpre_ref, w_ref, wdt_ref, lng_ref, sw_ref, sb_ref,
                 a_ref, z_ref, xbc_ref, dt_ref, q_ref, k_ref, v_ref):
    hn = _rms(h_ref[...], gpre_ref[...]).astype(BF16)

    def proj(lo, hi):
        return _dot(hn, w_ref[:, lo:hi])

    z_ref[...] = proj(OFF_Z, OFF_XBC)
    xbc_ref[...] = proj(OFF_XBC, OFF_DT)
    q_ref[...] = proj(OFF_DT, OFF_DT + ATTN_WIDTH)
    k_ref[...] = proj(OFF_DT + ATTN_WIDTH, OFF_DT + 2 * ATTN_WIDTH)
    v_ref[...] = proj(OFF_DT + 2 * ATTN_WIDTH, W_MAIN)
    dt_ref[...] = _dot(hn, wdt_ref[...])

    uv = jax.nn.gelu(proj(0, OFF_Z))
    u = uv[:, :SGU_WIDTH]
    v = uv[:, SGU_WIDTH:]
    mu = jnp.mean(v, axis=-1, keepdims=True)
    var = jnp.mean(jnp.square(v - mu), axis=-1, keepdims=True)
    vn = ((v - mu) * lax.rsqrt(var + EPS) * lng_ref[...]).astype(BF16)
    causal = (lax.broadcasted_iota(jnp.int32, (CHUNK, CHUNK), 0)
              >= lax.broadcasted_iota(jnp.int32, (CHUNK, CHUNK), 1))
    lane_group = lax.broadcasted_iota(jnp.int32, (1, SGU_WIDTH), 1) // HEAD_DIM
    ws = [jnp.where(causal, sw_ref[g], 0.0).astype(BF16) for g in range(SGU_GROUPS)]
    for c in range(ROW_TILE // CHUNK):
        r0 = c * CHUNK
        vc = vn[r0:r0 + CHUNK]
        s = sb_ref[...]
        for g in range(SGU_GROUPS):
            s = s + jnp.where(lane_group == g, _dot(ws[g], vc), 0.0)
        a_ref[r0:r0 + CHUNK, :] = u[r0:r0 + CHUNK] * s


def _inproj(h, gpre, w_main, w_dt, lng, sgu_w, sgu_b, layer):
    rows, d = h.shape
    row_spec = lambda w: pl.BlockSpec((ROW_TILE, w), lambda i: (i, 0))
    out_widths = (SGU_WIDTH, SSM_INNER, SSM_CONV_CH, SSM_INNER, ATTN_WIDTH, ATTN_WIDTH, ATTN_WIDTH)
    return pl.pallas_call(
        _inproj_body,
        out_shape=[jax.ShapeDtypeStruct((rows, w), F32) for w in out_widths],
        grid=(rows // ROW_TILE,),
        in_specs=[
            row_spec(d),
            _const_spec((None, None, 1, d), (layer, 1, 0, 0)),
            _const_spec((None, d, W_MAIN), (layer, 0, 0)),
            _const_spec((None, d, SSM_INNER), (layer, 0, 0)),
            _const_spec((None, 1, SGU_WIDTH), (layer, 0, 0)),
            _const_spec((None, SGU_GROUPS, CHUNK, CHUNK), (layer, 0, 0, 0)),
            _const_spec((None, CHUNK, SGU_WIDTH), (layer, 0, 0)),
        ],
        out_specs=[row_spec(w) for w in out_widths],
        compiler_params=_params(1),
        name="inproj_sgu",
    )(h, gpre, w_main, w_dt, lng, sgu_w, sgu_b)


def _softplus(x):
    return jnp.maximum(x, 0.0) + jnp.log1p(jnp.exp(-jnp.abs(x)))


def _cumsum_rows(tril, a):
    hi = a.astype(BF16)
    r1 = a - hi.astype(F32)
    mid = r1.astype(BF16)
    lo = (r1 - mid.astype(F32)).astype(BF16)
    return _dot(tril, hi) + _dot(tril, mid) + _dot(tril, lo)


def _ssd_body(z_ref, xbc_ref, dt_ref, cw_ref, cb_ref, dtb_ref, alog_ref, d_ref, ng_ref,
              o_ref, ext_ref, hs_ref):
    t = pl.program_id(1)

    @pl.when(t == 0)
    def _():
        ext_ref[0:CONV_PAD, :] = jnp.zeros((CONV_PAD, SSM_CONV_CH), F32)
        hs_ref[...] = jnp.zeros_like(hs_ref)

    @pl.when(t > 0)
    def _():
        ext_ref[0:CONV_PAD, :] = ext_ref[SSD_TILE:SSD_TILE + CONV_PAD, :]

    ext_ref[CONV_PAD:CONV_PAD + SSD_TILE, :] = xbc_ref[0]

    row = lax.broadcasted_iota(jnp.int32, (CHUNK, CHUNK), 0)
    col = lax.broadcasted_iota(jnp.int32, (CHUNK, CHUNK), 1)
    causal = row >= col
    tril = jnp.where(causal, 1.0, 0.0).astype(BF16)
    low_half = col < HEAD_DIM
    lane_head = lax.broadcasted_iota(jnp.int32, (1, GROUP_W), 1) // HEAD_DIM
    a_neg = -jnp.exp(alog_ref[...])

    def conv(r0, lo, hi):
        acc = cb_ref[:, lo:hi]
        for k in range(SSM_CONV):
            start = CONV_PAD + r0 - (SSM_CONV - 1) + k
            acc = acc + ext_ref[start:start + CHUNK, lo:hi] * cw_ref[k:k + 1, lo:hi]
        return _silu(acc)

    for c in range(SSD_TILE // CHUNK):
        r0 = c * CHUNK
        xs = conv(r0, 0, SSM_INNER)
        bm = conv(r0, SSM_INNER, SSM_INNER + SSM_GROUPS * SSM_STATE)
        cm = conv(r0, SSM_INNER + SSM_GROUPS * SSM_STATE, SSM_CONV_CH)
        dt = _softplus(dt_ref[0, r0:r0 + CHUNK, :] + dtb_ref[...])
        acs = _cumsum_rows(tril, dt * a_neg)
        acs_last = acs[CHUNK - 1:CHUNK, :]
        eacs = jnp.exp(acs)
        x = xs * dt
        xb = x.astype(BF16)
        xd = (x * jnp.exp(acs_last - acs)).astype(BF16)
        for g in range(SSM_GROUPS):
            gl = slice(g * GROUP_W, (g + 1) * GROUP_W)
            bg = bm[:, g * SSM_STATE:(g + 1) * SSM_STATE]
            cg = cm[:, g * SSM_STATE:(g + 1) * SSM_STATE].astype(BF16)
            cbm = _dot_nt(cg, bg.astype(BF16))
            xg = xb[:, gl]
            yg = jnp.zeros((CHUNK, GROUP_W), F32)
            for e in range(SSM_HEADS // SSM_GROUPS):
                head = g * (SSM_HEADS // SSM_GROUPS) + e
                pair = acs[:, (head // 2) * 2 * HEAD_DIM:(head // 2 + 1) * 2 * HEAD_DIM]
                swapped = pltpu.roll(pair, HEAD_DIM, 1)
                acs_col = jnp.where(low_half, pair, swapped) if head % 2 == 0 else jnp.where(low_half, swapped, pair)
                seg = acs_col - acs_col.T
                decay = jnp.exp(jnp.where(causal, seg, -jnp.inf))
                yh = _dot((cbm * decay).astype(BF16), xg)
                yg = jnp.where(lane_head == e, yh, yg)
            h_prev = hs_ref[g]
            y_off = _dot(cg, h_prev.astype(BF16)) * eacs[:, gl]
            hs_ref[g] = h_prev * eacs[CHUNK - 1:CHUNK, gl] + _dot(bg.T.astype(BF16), xd[:, gl])
            y = yg + y_off + d_ref[:, gl] * xs[:, gl]
            y = y * _silu(z_ref[0, r0:r0 + CHUNK, gl])
            o_ref[0, r0:r0 + CHUNK, gl] = _rms(y, ng_ref[:, gl])


def _ssd(z, xbc, dt, conv_w, conv_b, dtb, alog, dskip, norm_g, layer):
    b, s, _ = z.shape
    tile = lambda w: pl.BlockSpec((1, SSD_TILE, w), lambda i, j: (i, j, 0))
    vec = lambda w: _const_spec((None, 1, w), (layer, 0, 0))
    return pl.pallas_call(
        _ssd_body,
        out_shape=jax.ShapeDtypeStruct((b, s, SSM_INNER), F32),
        grid=(b, s // SSD_TILE),
        in_specs=[
            tile(SSM_INNER), tile(SSM_CONV_CH), tile(SSM_INNER),
            _const_spec((None, SSM_CONV, SSM_CONV_CH), (layer, 0, 0)),
            vec(SSM_CONV_CH), vec(SSM_INNER), vec(SSM_INNER), vec(SSM_INNER), vec(SSM_INNER),
        ],
        out_specs=tile(SSM_INNER),
        scratch_shapes=[
            pltpu.VMEM((CONV_PAD + SSD_TILE, SSM_CONV_CH), F32),
            pltpu.VMEM((SSM_GROUPS, SSM_STATE, GROUP_W), F32),
        ],
        compiler_params=_params(2),
        name="ssd",
    )(z, xbc, dt, conv_w, conv_b, dtb, alog, dskip, norm_g)


def _attn_body(q_ref, kp_ref, kc_ref, vp_ref, vc_ref, bias_ref, o_ref, lse_ref, kbuf, vbuf, *, tq):
    n = pl.program_id(2)
    kbuf[0:CHUNK, :] = kp_ref[0].astype(BF16)
    kbuf[CHUNK:CHUNK + tq, :] = kc_ref[0].astype(BF16)
    vbuf[0:CHUNK, :] = vp_ref[0].astype(BF16)
    vbuf[CHUNK:CHUNK + tq, :] = vc_ref[0].astype(BF16)
    lane_head = lax.broadcasted_iota(jnp.int32, (1, ATTN_WIDTH), 1) // HEAD_DIM
    key_col = lax.broadcasted_iota(jnp.int32, (CHUNK, 2 * CHUNK), 1)
    nblk = tq // CHUNK

    def block(j, carry):
        r0 = pl.multiple_of(j * CHUNK, CHUNK)
        q = q_ref[0, pl.ds(r0, CHUNK), :] * (HEAD_DIM ** -0.5)
        kk = kbuf[pl.ds(r0, 2 * CHUNK), :]
        vv = vbuf[pl.ds(r0, 2 * CHUNK), :]
        valid = key_col >= jnp.where(n * nblk + j > 0, 0, CHUNK)
        acc = jnp.zeros((CHUNK, ATTN_WIDTH), F32)
        lse = jnp.zeros((CHUNK, ATTN_WIDTH), F32)
        for h in range(ATTN_HEADS):
            sel = lane_head == h
            qh = jnp.where(sel, q, 0.0).astype(BF16)
            logits = jnp.where(valid, _dot_nt(qh, kk) + bias_ref[h], -jnp.inf)
            m = jnp.max(logits, axis=-1, keepdims=True)
            p = jnp.exp(logits - m)
            s = jnp.sum(p, axis=-1, keepdims=True)
            oh = _dot(p.astype(BF16), vv) * (1.0 / s)
            acc = jnp.where(sel, oh, acc)
            lse = jnp.where(sel, m + jnp.log(s), lse)
        o_ref[0, pl.ds(r0, CHUNK), :] = acc
        lse_ref[0, pl.ds(r0, CHUNK), :] = lse
        return carry

    lax.fori_loop(0, nblk, block, 0)


def _attn_branch(q, k, v, bias, dil):
    b, s, w = q.shape
    cls_len = s // dil
    tq = min(ATTN_TILE, cls_len)
    qr, kr, vr = (t.reshape(b, cls_len, dil * w) for t in (q, k, v))
    cur = pl.BlockSpec((1, tq, w), lambda i, r, n: (i, n, r))
    prev = pl.BlockSpec((1, CHUNK, w), lambda i, r, n: (i, jnp.maximum(n * (tq // CHUNK) - 1, 0), r))
    o, lse = pl.pallas_call(
        functools.partial(_attn_body, tq=tq),
        out_shape=[jax.ShapeDtypeStruct((b, cls_len, dil * w), F32)] * 2,
        grid=(b, dil, cls_len // tq),
        in_specs=[cur, prev, cur, prev, cur, _const_spec((ATTN_HEADS, CHUNK, 2 * CHUNK), (0, 0, 0))],
        out_specs=[cur, cur],
        scratch_shapes=[pltpu.VMEM((CHUNK + tq, w), BF16)] * 2,
        compiler_params=_params(3),
        name=f"dilated_attn_{dil}",
    )(qr, kr, kr, vr, vr, bias)
    return o.reshape(b, s, w), lse.reshape(b, s, w)


def _t5_bucket(dist):
    max_exact = REL_BUCKETS // 2
    d = jnp.maximum(dist, 1).astype(F32)
    large = max_exact + (jnp.log(d / max_exact) / math.log(REL_MAX_DIST / max_exact)
                         * (REL_BUCKETS - max_exact)).astype(jnp.int32)
    large = jnp.minimum(large, REL_BUCKETS - 1)
    return jnp.where(dist < max_exact, dist, large)


def _branch_bias(rel_bias, window, dil):
    span = window // dil
    assert span <= CHUNK
    dist = jnp.arange(CHUNK)[:, None] + CHUNK - jnp.arange(2 * CHUNK)[None, :]
    band = (dist >= 0) & (dist <= span)
    bias = jnp.transpose(rel_bias[_t5_bucket(jnp.maximum(dist, 0) * dil)], (2, 0, 1))
    return jnp.where(band[None], bias.astype(F32), -jnp.inf)


def _mixout_body(h_ref, a_ref, b_ref, o1_ref, o2_ref, o3_ref, l1_ref, l2_ref, l3_ref, w_ref, gpost_ref, out_ref):
    l1, l2, l3 = l1_ref[...], l2_ref[...], l3_ref[...]
    m = jnp.maximum(jnp.maximum(l1, l2), l3)
    e1, e2, e3 = jnp.exp(l1 - m), jnp.exp(l2 - m), jnp.exp(l3 - m)
    c = (e1 * o1_ref[...] + e2 * o2_ref[...] + e3 * o3_ref[...]) * (1.0 / (e1 + e2 + e3))
    off_b = SGU_WIDTH
    off_c = SGU_WIDTH + SSM_INNER
    y = (_dot(a_ref[...].astype(BF16), w_ref[0:off_b, :])
         + _dot(b_ref[...].astype(BF16), w_ref[off_b:off_c, :])
         + _dot(c.astype(BF16), w_ref[off_c:off_c + ATTN_WIDTH, :]))
    out_ref[...] = h_ref[...] + _rms(y, gpost_ref[...])


def _mixout(h, a, bo, outs, lses, w_out, gpost, layer):
    rows, d = h.shape
    row_spec = lambda w: pl.BlockSpec((ROW_TILE, w), lambda i: (i, 0))
    attn = row_spec(ATTN_WIDTH)
    return pl.pallas_call(
        _mixout_body,
        out_shape=jax.ShapeDtypeStruct((rows, d), F32),
        grid=(rows // ROW_TILE,),
        in_specs=[row_spec(d), row_spec(SGU_WIDTH), row_spec(SSM_INNER), attn, attn, attn, attn, attn, attn,
                  _const_spec((None, d, d), (layer, 0, 0)),
                  _const_spec((None, None, 1, d), (layer, 1, 0, 0))],
        out_specs=row_spec(d),
        compiler_params=_params(1),
        name="mix_out",
    )(h, a, bo, *outs, *lses, w_out, gpost)


def _memkv_body(mem_ref, g_ref, w_ref, k_ref, v_ref):
    d = mem_ref.shape[-1]
    mn = _rms(mem_ref[0], g_ref[...]).astype(BF16)
    k_ref[0] = _dot(mn, w_ref[:, 0:d]).astype(BF16)
    v_ref[0] = _dot(mn, w_ref[:, d:2 * d]).astype(BF16)


def _memkv(mem, g, wkv, layer):
    b, m, d = mem.shape
    blk = pl.BlockSpec((1, m, d), lambda i: (i, 0, 0))
    return pl.pallas_call(
        _memkv_body,
        out_shape=[jax.ShapeDtypeStruct((b, m, d), BF16)] * 2,
        grid=(b,),
        in_specs=[blk, _const_spec((None, 1, d), (layer, 0, 0)), _const_spec((None, d, 2 * d), (layer, 0, 0))],
        out_specs=[blk, blk],
        compiler_params=_params(1),
        name="mem_kv",
    )(mem, g, wkv)


def _xattn_body(h_ref, gpre_ref, wq_ref, k_ref, v_ref, wo_ref, gpost_ref, o_ref):
    h = h_ref[0]
    d = h.shape[-1]
    hd = d // XATTN_HEADS
    hn = _rms(h, gpre_ref[...]).astype(BF16)
    q = (_dot(hn, wq_ref[...]) * (hd ** -0.5)).astype(BF16)
    outs = []
    for i in range(XATTN_HEADS):
        hs = slice(i * hd, (i + 1) * hd)
        logits = _dot_nt(q[:, hs], k_ref[0, :, hs])
        m = jnp.max(logits, axis=-1, keepdims=True)
        p = jnp.exp(logits - m)
        s = jnp.sum(p, axis=-1, keepdims=True)
        outs.append((_dot(p.astype(BF16), v_ref[0, :, hs]) * (1.0 / s)).astype(BF16))
    y = _dot(jnp.concatenate(outs, axis=-1), wo_ref[...])
    o_ref[0] = h + _rms(y, gpost_ref[...])


def _xattn(h, gpre, wq, k, v, wo, gpost, layer):
    b, s, d = h.shape
    m = k.shape[1]
    tile = pl.BlockSpec((1, ROW_TILE, d), lambda i, j: (i, j, 0))
    kv = pl.BlockSpec((1, m, d), lambda i, j: (i, 0, 0))
    return pl.pallas_call(
        _xattn_body,
        out_shape=jax.ShapeDtypeStruct((b, s, d), F32),
        grid=(b, s // ROW_TILE),
        in_specs=[tile, _const_spec((None, None, 1, d), (layer, 2, 0, 0)), _const_spec((None, d, d), (layer, 0, 0)),
                  kv, kv, _const_spec((None, d, d), (layer, 0, 0)), _const_spec((None, None, 1, d), (layer, 2, 0, 0))],
        out_specs=tile,
        compiler_params=_params(2),
        name="mem_xattn",
    )(h, gpre, wq, k, v, wo, gpost)


def kernel(x, mem, norm_pre, norm_post, ffn_wi, ffn_wo, mix_w_in, mix_w_out, sgu_ln_g, sgu_w, sgu_b, ssm_conv_w, ssm_conv_b, ssm_dt_bias, ssm_a_log, ssm_d, ssm_norm_g, rel_bias, mem_norm_g, xattn_wq, xattn_wkv, xattn_wo):
    b, s, d = x.shape
    depth = norm_pre.shape[0]
    assert s % (ATTN_TILE * 1) == 0 and (b * s) % ROW_TILE == 0 and s % SSD_TILE == 0
    assert all(s % (dil * CHUNK) == 0 for _, dil in DILATED_PATTERNS)

    npre = norm_pre[:, :, None, :]
    npost = norm_post[:, :, None, :]
    wi = ffn_wi.astype(BF16)
    wo = ffn_wo.astype(BF16)
    w_main = jnp.concatenate([mix_w_in[..., :OFF_DT], mix_w_in[..., OFF_QKV:]], axis=-1).astype(BF16)
    w_dt = jnp.repeat(mix_w_in[..., OFF_DT:OFF_QKV], HEAD_DIM, axis=-1).astype(BF16)
    w_out = mix_w_out.astype(BF16)
    lng = sgu_ln_g[:, None, :]
    sgu_b_e = jnp.repeat(jnp.swapaxes(sgu_b, 1, 2), HEAD_DIM, axis=-1)
    per_head = lambda p: jnp.repeat(p, HEAD_DIM, axis=-1)[:, None, :]
    dtb, alog, dskip = per_head(ssm_dt_bias), per_head(ssm_a_log), per_head(ssm_d)
    conv_b = ssm_conv_b[:, None, :]
    norm_g = ssm_norm_g[:, None, :]
    mem_g = mem_norm_g[:, None, :]
    wq, wkv, wxo = xattn_wq.astype(BF16), xattn_wkv.astype(BF16), xattn_wo.astype(BF16)
    biases = [_branch_bias(rel_bias, window, dil) for window, dil in DILATED_PATTERNS]

    h = x.reshape(b * s, d)
    for l in range(depth):
        h = _ffn(h, npre, wi, wo, npost, l, 0)
        a, z, xbc, dt, q, k, v = _inproj(h, npre, w_main, w_dt, lng, sgu_w, sgu_b_e, l)
        seq = lambda t: t.reshape(b, s, t.shape[-1])
        bo = _ssd(seq(z), seq(xbc), seq(dt), ssm_conv_w, conv_b, dtb, alog, dskip, norm_g, l)
        outs, lses = [], []
        for bias, (_, dil) in zip(biases, DILATED_PATTERNS):
            o, lse = _attn_branch(seq(q), seq(k), seq(v), bias, dil)
            outs.append(o.reshape(b * s, ATTN_WIDTH))
            lses.append(lse.reshape(b * s, ATTN_WIDTH))
        h = _mixout(h, a, bo.reshape(b * s, SSM_INNER), outs, lses, w_out, npost, l)
        mk, mv = _memkv(mem, mem_g, wkv, l)
        h = _xattn(h.reshape(b, s, d), npre, wq, mk, mv, wxo, npost, l).reshape(b * s, d)
        h = _ffn(h, npre, wi, wo, npost, l, 1)
    return h.reshape(b, s, d)
```

```python
import functools
import math

import jax
import jax.numpy as jnp
from jax import lax
from jax.experimental import pallas as pl
from jax.experimental.pallas import tpu as pltpu

F32 = jnp.float32
BF16 = jnp.bfloat16
EPS = 1e-6
LOG2E = math.log2(math.e)
LN2 = math.log(2.0)

HEAD_DIM = 64
CHUNK = 128
SGU_WIDTH = 256
SGU_GROUPS = SGU_WIDTH // HEAD_DIM
SSM_INNER = 512
SSM_HEADS = SSM_INNER // HEAD_DIM
SSM_GROUPS = 2
SSM_STATE = 128
SSM_CONV = 4
SSM_CONV_CH = SSM_INNER + 2 * SSM_GROUPS * SSM_STATE
GROUP_W = SSM_INNER // SSM_GROUPS
ATTN_WIDTH = 256
ATTN_HEADS = ATTN_WIDTH // HEAD_DIM
DILATED_PATTERNS = ((128, 1), (512, 4), (2048, 16))
REL_BUCKETS = 32
REL_MAX_DIST = 2048
XATTN_HEADS = 4

OFF_Z = 2 * SGU_WIDTH
OFF_XBC = OFF_Z + SSM_INNER
OFF_DT = OFF_XBC + SSM_CONV_CH
OFF_QKV = OFF_DT + SSM_HEADS
W_MAIN = OFF_DT + 3 * ATTN_WIDTH

ROW_TILE = 512
FF_TILE = 256
SSD_TILE = 512
ATTN_TILE = 1024
CONV_PAD = 8
VMEM_LIMIT = 56 * 1024 * 1024


def _params(n_axes):
    return pltpu.CompilerParams(dimension_semantics=("arbitrary",) * n_axes,
                                vmem_limit_bytes=VMEM_LIMIT)


def _rms(x, g):
    return x * lax.rsqrt(jnp.mean(x * x, axis=-1, keepdims=True) + EPS) * g


def _silu(x):
    return x * jax.nn.sigmoid(x)


def _dot(a, b):
    return jnp.dot(a, b, preferred_element_type=F32)


def _dot_nt(a, b):
    return lax.dot_general(a, b, (((1,), (1,)), ((), ())), preferred_element_type=F32)


def _const_spec(shape, index):
    return pl.BlockSpec(shape, lambda *_: index)


def _ffn_body(h_ref, gpre_ref, wi_ref, wo_ref, gpost_ref, o_ref, acc_ref, *, d_ff):
    h = h_ref[...]
    hn = _rms(h, gpre_ref[...]).astype(BF16)
    for c in range(d_ff // FF_TILE):
        lo = c * FF_TILE
        g = _dot(hn, wi_ref[:, lo:lo + FF_TILE])
        u = _dot(hn, wi_ref[:, d_ff + lo:d_ff + lo + FF_TILE])
        y = _dot((_silu(g) * u).astype(BF16), wo_ref[lo:lo + FF_TILE, :])
        if c == 0:
            acc_ref[...] = y
        else:
            acc_ref[...] += y
    o_ref[...] = h + 0.5 * _rms(acc_ref[...], gpost_ref[...])


def _ffn(h, gpre, wi, wo, gpost, layer, which):
    rows, d = h.shape
    d_ff = wo.shape[2]
    return pl.pallas_call(
        functools.partial(_ffn_body, d_ff=d_ff),
        out_shape=jax.ShapeDtypeStruct((rows, d), F32),
        grid=(rows // ROW_TILE,),
        in_specs=[
            pl.BlockSpec((ROW_TILE, d), lambda i: (i, 0)),
            _const_spec((None, None, 1, d), (layer, 3 * which, 0, 0)),
            _const_spec((None, None, d, 2 * d_ff), (layer, which, 0, 0)),
            _const_spec((None, None, d_ff, d), (layer, which, 0, 0)),
            _const_spec((None, None, 1, d), (layer, 3 * which, 0, 0)),
        ],
        out_specs=pl.BlockSpec((ROW_TILE, d), lambda i: (i, 0)),
        scratch_shapes=[pltpu.VMEM((ROW_TILE, d), F32)],
        compiler_params=_params(1),
        name="ffn",
    )(h, gpre, wi, wo, gpost)


def _inproj_body(h_ref, gpre_ref, w_ref, wdt_ref, lng_ref, sw_ref, sb_ref,
                 a_ref, z_ref, xbc_ref, dt_ref, q_ref, k_ref, v_ref):
    hn = _rms(h_ref[...], gpre_ref[...]).astype(BF16)

    def proj(lo, hi):
        return _dot(hn, w_ref[:, lo:hi])

    z_ref[...] = proj(OFF_Z, OFF_XBC)
    xbc_ref[...] = proj(OFF_XBC, OFF_DT)
    q_ref[...] = proj(OFF_DT, OFF_DT + ATTN_WIDTH)
    k_ref[...] = proj(OFF_DT + ATTN_WIDTH, OFF_DT + 2 * ATTN_WIDTH)
    v_ref[...] = proj(OFF_DT + 2 * ATTN_WIDTH, W_MAIN)
    dt_ref[...] = _dot(hn, wdt_ref[...])

    uv = jax.nn.gelu(proj(0, OFF_Z))
    u = uv[:, :SGU_WIDTH]
    v = uv[:, SGU_WIDTH:]
    mu = jnp.mean(v, axis=-1, keepdims=True)
    var = jnp.mean(jnp.square(v - mu), axis=-1, keepdims=True)
    vn = ((v - mu) * lax.rsqrt(var + EPS) * lng_ref[...]).astype(BF16)
    causal = (lax.broadcasted_iota(jnp.int32, (CHUNK, CHUNK), 0)
              >= lax.broadcasted_iota(jnp.int32, (CHUNK, CHUNK), 1))
    lane_group = lax.broadcasted_iota(jnp.int32, (1, SGU_WIDTH), 1) // HEAD_DIM
    ws = [jnp.where(causal, sw_ref[g], 0.0).astype(BF16) for g in range(SGU_GROUPS)]
    for c in range(ROW_TILE // CHUNK):
        r0 = c * CHUNK
        vc = vn[r0:r0 + CHUNK]
        s = sb_ref[...]
        for g in range(SGU_GROUPS):
            s = s + jnp.where(lane_group == g, _dot(ws[g], vc), 0.0)
        a_ref[r0:r0 + CHUNK, :] = u[r0:r0 + CHUNK] * s


def _inproj(h, gpre, w_main, w_dt, lng, sgu_w, sgu_b, layer):
    rows, d = h.shape
    row_spec = lambda w: pl.BlockSpec((ROW_TILE, w), lambda i: (i, 0))
    out_widths = (SGU_WIDTH, SSM_INNER, SSM_CONV_CH, SSM_INNER, ATTN_WIDTH, ATTN_WIDTH, ATTN_WIDTH)
    return pl.pallas_call(
        _inproj_body,
        out_shape=[jax.ShapeDtypeStruct((rows, w), F32) for w in out_widths],
        grid=(rows // ROW_TILE,),
        in_specs=[
            row_spec(d),
            _const_spec((None, None, 1, d), (layer, 1, 0, 0)),
            _const_spec((None, d, W_MAIN), (layer, 0, 0)),
            _const_spec((None, d, SSM_INNER), (layer, 0, 0)),
            _const_spec((None, 1, SGU_WIDTH), (layer, 0, 0)),
            _const_spec((None, SGU_GROUPS, CHUNK, CHUNK), (layer, 0, 0, 0)),
            _const_spec((None, CHUNK, SGU_WIDTH), (layer, 0, 0)),
        ],
        out_specs=[row_spec(w) for w in out_widths],
        compiler_params=_params(1),
        name="inproj_sgu",
    )(h, gpre, w_main, w_dt, lng, sgu_w, sgu_b)


def _softplus(x):
    return jnp.maximum(x, 0.0) + jnp.log1p(jnp.exp(-jnp.abs(x)))


def _cumsum_rows(tril, a):
    hi = a.astype(BF16)
    r1 = a - hi.astype(F32)
    mid = r1.astype(BF16)
    lo = (r1 - mid.astype(F32)).astype(BF16)
    return _dot(tril, hi) + _dot(tril, mid) + _dot(tril, lo)


def _ssd_body(z_ref, xbc_ref, dt_ref, cw_ref, cb_ref, dtb_ref, alog_ref, d_ref, ng_ref,
              o_ref, ext_ref, hs_ref):
    t = pl.program_id(1)

    @pl.when(t == 0)
    def _():
        ext_ref[0:CONV_PAD, :] = jnp.zeros((CONV_PAD, SSM_CONV_CH), F32)
        hs_ref[...] = jnp.zeros_like(hs_ref)

    @pl.when(t > 0)
    def _():
        ext_ref[0:CONV_PAD, :] = ext_ref[SSD_TILE:SSD_TILE + CONV_PAD, :]

    ext_ref[CONV_PAD:CONV_PAD + SSD_TILE, :] = xbc_ref[0]

    row = lax.broadcasted_iota(jnp.int32, (CHUNK, CHUNK), 0)
    col = lax.broadcasted_iota(jnp.int32, (CHUNK, CHUNK), 1)
    causal = row >= col
    tril = jnp.where(causal, 1.0, 0.0).astype(BF16)
    low_half = col < HEAD_DIM
    lane_head = lax.broadcasted_iota(jnp.int32, (1, GROUP_W), 1) // HEAD_DIM
    a_neg = -jnp.exp(alog_ref[...])

    def conv(r0, lo, hi):
        acc = cb_ref[:, lo:hi]
        for k in range(SSM_CONV):
            start = CONV_PAD + r0 - (SSM_CONV - 1) + k
            acc = acc + ext_ref[start:start + CHUNK, lo:hi] * cw_ref[k:k + 1, lo:hi]
        return _silu(acc)

    for c in range(SSD_TILE // CHUNK):
        r0 = c * CHUNK
        xs = conv(r0, 0, SSM_INNER)
        bm = conv(r0, SSM_INNER, SSM_INNER + SSM_GROUPS * SSM_STATE)
        cm = conv(r0, SSM_INNER + SSM_GROUPS * SSM_STATE, SSM_CONV_CH)
        dt = _softplus(dt_ref[0, r0:r0 + CHUNK, :] + dtb_ref[...])
        acs = _cumsum_rows(tril, dt * a_neg)
        acs_last = acs[CHUNK - 1:CHUNK, :]
        eacs = jnp.exp(acs)
        x = xs * dt
        xb = x.astype(BF16)
        xd = (x * jnp.exp(acs_last - acs)).astype(BF16)
        for g in range(SSM_GROUPS):
            gl = slice(g * GROUP_W, (g + 1) * GROUP_W)
            bg = bm[:, g * SSM_STATE:(g + 1) * SSM_STATE]
            cg = cm[:, g * SSM_STATE:(g + 1) * SSM_STATE].astype(BF16)
            cbm = _dot_nt(cg, bg.astype(BF16))
            xg = xb[:, gl]
            yg = jnp.zeros((CHUNK, GROUP_W), F32)
            for e in range(SSM_HEADS // SSM_GROUPS):
                head = g * (SSM_HEADS // SSM_GROUPS) + e
                pair = acs[:, (head // 2) * 2 * HEAD_DIM:(head // 2 + 1) * 2 * HEAD_DIM]
                swapped = pltpu.roll(pair, HEAD_DIM, 1)
                acs_col = jnp.where(low_half, pair, swapped) if head % 2 == 0 else jnp.where(low_half, swapped, pair)
                seg = acs_col - acs_col.T
                decay = jnp.exp(jnp.where(causal, seg, -jnp.inf))
                yh = _dot((cbm * decay).astype(BF16), xg)
                yg = jnp.where(lane_head == e, yh, yg)
            h_prev = hs_ref[g]
            y_off = _dot(cg, h_prev.astype(BF16)) * eacs[:, gl]
            hs_ref[g] = h_prev * eacs[CHUNK - 1:CHUNK, gl] + _dot(bg.T.astype(BF16), xd[:, gl])
            y = yg + y_off + d_ref[:, gl] * xs[:, gl]
            y = y * _silu(z_ref[0, r0:r0 + CHUNK, gl])
            o_ref[0, r0:r0 + CHUNK, gl] = _rms(y, ng_ref[:, gl])


def _ssd(z, xbc, dt, conv_w, conv_b, dtb, alog, dskip, norm_g, layer):
    b, s, _ = z.shape
    tile = lambda w: pl.BlockSpec((1, SSD_TILE, w), lambda i, j: (i, j, 0))
    vec = lambda w: _const_spec((None, 1, w), (layer, 0, 0))
    return pl.pallas_call(
        _ssd_body,
        out_shape=jax.ShapeDtypeStruct((b, s, SSM_INNER), F32),
        grid=(b, s // SSD_TILE),
        in_specs=[
            tile(SSM_INNER), tile(SSM_CONV_CH), tile(SSM_INNER),
            _const_spec((None, SSM_CONV, SSM_CONV_CH), (layer, 0, 0)),
            vec(SSM_CONV_CH), vec(SSM_INNER), vec(SSM_INNER), vec(SSM_INNER), vec(SSM_INNER),
        ],
        out_specs=tile(SSM_INNER),
        scratch_shapes=[
            pltpu.VMEM((CONV_PAD + SSD_TILE, SSM_CONV_CH), F32),
            pltpu.VMEM((SSM_GROUPS, SSM_STATE, GROUP_W), F32),
        ],
        compiler_params=_params(2),
        name="ssd",
    )(z, xbc, dt, conv_w, conv_b, dtb, alog, dskip, norm_g)


def _attn_body(q_ref, kp_ref, kc_ref, vp_ref, vc_ref, bias_ref, o_ref, lse_ref, kbuf, vbuf, *, tq, ncls):
    n = pl.program_id(2)
    nblk = tq // CHUNK
    w = ATTN_WIDTH
    lane_head = lax.broadcasted_iota(jnp.int32, (1, w), 1) // HEAD_DIM
    for c in range(ncls):
        cl = slice(c * w, (c + 1) * w)
        kbuf[c, 0:CHUNK, :] = kp_ref[0, :, cl].astype(BF16)
        kbuf[c, CHUNK:CHUNK + tq, :] = kc_ref[0, :, cl].astype(BF16)
        vbuf[c, 0:CHUNK, :] = vp_ref[0, :, cl].astype(BF16)
        vbuf[c, CHUNK:CHUNK + tq, :] = vc_ref[0, :, cl].astype(BF16)
    first_table = jnp.where(n == 0, 1, 0)
    for c in range(ncls):
        cl = slice(c * w, (c + 1) * w)
        for j in range(nblk):
            rows = slice(j * CHUNK, (j + 1) * CHUNK)
            q = q_ref[0, rows, cl] * (HEAD_DIM ** -0.5 * LOG2E)
            lhs = jnp.concatenate([jnp.where(lane_head == h, q, 0.0) for h in range(ATTN_HEADS)], axis=0).astype(BF16)
            kk = kbuf[c, j * CHUNK:(j + 2) * CHUNK, :]
            vv = vbuf[c, j * CHUNK:(j + 2) * CHUNK, :]
            logits = _dot_nt(lhs, kk) + bias_ref[first_table if j == 0 else 0]
            m = jnp.max(logits, axis=-1, keepdims=True)
            p = jnp.exp2(logits - m)
            s = jnp.sum(p, axis=-1, keepdims=True)
            pv = _dot(p.astype(BF16), vv) * (1.0 / s)
            lse = (m + jnp.log2(s)) * LN2
            o_acc = pv[0:CHUNK]
            lse_acc = jnp.broadcast_to(lse[0:CHUNK], (CHUNK, w))
            for h in range(1, ATTN_HEADS):
                hr = slice(h * CHUNK, (h + 1) * CHUNK)
                o_acc = jnp.where(lane_head == h, pv[hr], o_acc)
                lse_acc = jnp.where(lane_head == h, lse[hr], lse_acc)
            o_ref[0, rows, cl] = o_acc
            lse_ref[0, rows, cl] = lse_acc


def _attn_branch(q, k, v, biases, branch):
    dil = DILATED_PATTERNS[branch][1]
    b, s, w = q.shape
    cls_len = s // dil
    tq = min(ATTN_TILE, cls_len)
    ncls = min(dil, ATTN_TILE // tq)
    qr, kr, vr = (t.reshape(b, cls_len, dil * w) for t in (q, k, v))
    cur = pl.BlockSpec((1, tq, ncls * w), lambda i, r, n: (i, n, r))
    prev = pl.BlockSpec((1, CHUNK, ncls * w), lambda i, r, n: (i, jnp.maximum(n * (tq // CHUNK) - 1, 0), r))
    o, lse = pl.pallas_call(
        functools.partial(_attn_body, tq=tq, ncls=ncls),
        out_shape=[jax.ShapeDtypeStruct((b, cls_len, dil * w), F32)] * 2,
        grid=(b, dil // ncls, cls_len // tq),
        in_specs=[cur, prev, cur, prev, cur, _const_spec((None, 2, ATTN_HEADS * CHUNK, 2 * CHUNK), (branch, 0, 0, 0))],
        out_specs=[cur, cur],
        scratch_shapes=[pltpu.VMEM((ncls, CHUNK + tq, w), BF16)] * 2,
        compiler_params=_params(3),
        name=f"dilated_attn_{dil}",
    )(qr, kr, kr, vr, vr, biases)
    return o.reshape(b, s, w), lse.reshape(b, s, w)


def _t5_bucket(dist):
    max_exact = REL_BUCKETS // 2
    d = jnp.maximum(dist, 1).astype(F32)
    large = max_exact + (jnp.log(d / max_exact) / math.log(REL_MAX_DIST / max_exact)
                         * (REL_BUCKETS - max_exact)).astype(jnp.int32)
    large = jnp.minimum(large, REL_BUCKETS - 1)
    return jnp.where(dist < max_exact, dist, large)


def _bias_body(rel_ref, span_ref, bucket_ref, o_ref):
    bucket = bucket_ref[0]
    row = lax.broadcasted_iota(jnp.int32, (CHUNK, 2 * CHUNK), 0)
    col = lax.broadcasted_iota(jnp.int32, (CHUNK, 2 * CHUNK), 1)
    dist = row + CHUNK - col
    band = (dist >= 0) & (dist <= span_ref[pl.program_id(0)])
    band_cur = band & (col >= CHUNK)
    for h in range(ATTN_HEADS):
        bias = jnp.zeros((CHUNK, 2 * CHUNK), F32)
        for b in range(REL_BUCKETS):
            bias = jnp.where(bucket == b, rel_ref[b, h], bias)
        bias = bias * LOG2E
        o_ref[0, 0, h * CHUNK:(h + 1) * CHUNK, :] = jnp.where(band, bias, -jnp.inf)
        o_ref[0, 1, h * CHUNK:(h + 1) * CHUNK, :] = jnp.where(band_cur, bias, -jnp.inf)


def _bias_tables(rel_bias):
    assert all(window // dil <= CHUNK for window, dil in DILATED_PATTERNS)
    dist = jnp.arange(CHUNK)[:, None] + CHUNK - jnp.arange(2 * CHUNK)[None, :]
    buckets = jnp.stack([_t5_bucket(jnp.maximum(dist, 0) * dil) for _, dil in DILATED_PATTERNS]).astype(jnp.int32)
    spans = jnp.array([window // dil for window, dil in DILATED_PATTERNS], jnp.int32)
    nb = len(DILATED_PATTERNS)
    return pl.pallas_call(
        _bias_body,
        out_shape=jax.ShapeDtypeStruct((nb, 2, ATTN_HEADS * CHUNK, 2 * CHUNK), F32),
        grid=(nb,),
        in_specs=[pl.BlockSpec(memory_space=pltpu.SMEM), pl.BlockSpec(memory_space=pltpu.SMEM),
                  pl.BlockSpec((1, CHUNK, 2 * CHUNK), lambda i: (i, 0, 0))],
        out_specs=pl.BlockSpec((1, 2, ATTN_HEADS * CHUNK, 2 * CHUNK), lambda i: (i, 0, 0, 0)),
        compiler_params=_params(1),
        name="rel_bias_tables",
    )(rel_bias, spans, buckets)


def _mixout_body(h_ref, a_ref, b_ref, o1_ref, o2_ref, o3_ref, l1_ref, l2_ref, l3_ref, w_ref, gpost_ref, out_ref):
    l1, l2, l3 = l1_ref[...], l2_ref[...], l3_ref[...]
    m = jnp.maximum(jnp.maximum(l1, l2), l3)
    e1, e2, e3 = jnp.exp(l1 - m), jnp.exp(l2 - m), jnp.exp(l3 - m)
    c = (e1 * o1_ref[...] + e2 * o2_ref[...] + e3 * o3_ref[...]) * (1.0 / (e1 + e2 + e3))
    off_b = SGU_WIDTH
    off_c = SGU_WIDTH + SSM_INNER
    y = (_dot(a_ref[...].astype(BF16), w_ref[0:off_b, :])
         + _dot(b_ref[...].astype(BF16), w_ref[off_b:off_c, :])
         + _dot(c.astype(BF16), w_ref[off_c:off_c + ATTN_WIDTH, :]))
    out_ref[...] = h_ref[...] + _rms(y, gpost_ref[...])


def _mixout(h, a, bo, outs, lses, w_out, gpost, layer):
    rows, d = h.shape
    row_spec = lambda w: pl.BlockSpec((ROW_TILE, w), lambda i: (i, 0))
    attn = row_spec(ATTN_WIDTH)
    return pl.pallas_call(
        _mixout_body,
        out_shape=jax.ShapeDtypeStruct((rows, d), F32),
        grid=(rows // ROW_TILE,),
        in_specs=[row_spec(d), row_spec(SGU_WIDTH), row_spec(SSM_INNER), attn, attn, attn, attn, attn, attn,
                  _const_spec((None, d, d), (layer, 0, 0)),
                  _const_spec((None, None, 1, d), (layer, 1, 0, 0))],
        out_specs=row_spec(d),
        compiler_params=_params(1),
        name="mix_out",
    )(h, a, bo, *outs, *lses, w_out, gpost)


def _memkv_body(mem_ref, g_ref, w_ref, k_ref, v_ref):
    d = mem_ref.shape[-1]
    mn = _rms(mem_ref[0], g_ref[...]).astype(BF16)
    k_ref[0] = _dot(mn, w_ref[:, 0:d]).astype(BF16)
    v_ref[0] = _dot(mn, w_ref[:, d:2 * d]).astype(BF16)


def _memkv(mem, g, wkv, layer):
    b, m, d = mem.shape
    blk = pl.BlockSpec((1, m, d), lambda i: (i, 0, 0))
    return pl.pallas_call(
        _memkv_body,
        out_shape=[jax.ShapeDtypeStruct((b, m, d), BF16)] * 2,
        grid=(b,),
        in_specs=[blk, _const_spec((None, 1, d), (layer, 0, 0)), _const_spec((None, d, 2 * d), (layer, 0, 0))],
        out_specs=[blk, blk],
        compiler_params=_params(1),
        name="mem_kv",
    )(mem, g, wkv)


def _xattn_body(h_ref, gpre_ref, wq_ref, k_ref, v_ref, wo_ref, gpost_ref, o_ref):
    h = h_ref[0]
    d = h.shape[-1]
    hd = d // XATTN_HEADS
    hn = _rms(h, gpre_ref[...]).astype(BF16)
    q = (_dot(hn, wq_ref[...]) * (hd ** -0.5)).astype(BF16)
    outs = []
    for i in range(XATTN_HEADS):
        hs = slice(i * hd, (i + 1) * hd)
        logits = _dot_nt(q[:, hs], k_ref[0, :, hs])
        m = jnp.max(logits, axis=-1, keepdims=True)
        p = jnp.exp(logits - m)
        s = jnp.sum(p, axis=-1, keepdims=True)
        outs.append((_dot(p.astype(BF16), v_ref[0, :, hs]) * (1.0 / s)).astype(BF16))
    y = _dot(jnp.concatenate(outs, axis=-1), wo_ref[...])
    o_ref[0] = h + _rms(y, gpost_ref[...])


def _xattn(h, gpre, wq, k, v, wo, gpost, layer):
    b, s, d = h.shape
    m = k.shape[1]
    tile = pl.BlockSpec((1, ROW_TILE, d), lambda i, j: (i, j, 0))
    kv = pl.BlockSpec((1, m, d), lambda i, j: (i, 0, 0))
    return pl.pallas_call(
        _xattn_body,
        out_shape=jax.ShapeDtypeStruct((b, s, d), F32),
        grid=(b, s // ROW_TILE),
        in_specs=[tile, _const_spec((None, None, 1, d), (layer, 2, 0, 0)), _const_spec((None, d, d), (layer, 0, 0)),
                  kv, kv, _const_spec((None, d, d), (layer, 0, 0)), _const_spec((None, None, 1, d), (layer, 2, 0, 0))],
        out_specs=tile,
        compiler_params=_params(2),
        name="mem_xattn",
    )(h, gpre, wq, k, v, wo, gpost)


def kernel(x, mem, norm_pre, norm_post, ffn_wi, ffn_wo, mix_w_in, mix_w_out, sgu_ln_g, sgu_w, sgu_b, ssm_conv_w, ssm_conv_b, ssm_dt_bias, ssm_a_log, ssm_d, ssm_norm_g, rel_bias, mem_norm_g, xattn_wq, xattn_wkv, xattn_wo):
    b, s, d = x.shape
    depth = norm_pre.shape[0]
    assert s % (ATTN_TILE * 1) == 0 and (b * s) % ROW_TILE == 0 and s % SSD_TILE == 0
    assert all(s % (dil * CHUNK) == 0 for _, dil in DILATED_PATTERNS)

    npre = norm_pre[:, :, None, :]
    npost = norm_post[:, :, None, :]
    wi = ffn_wi.astype(BF16)
    wo = ffn_wo.astype(BF16)
    w_main = jnp.concatenate([mix_w_in[..., :OFF_DT], mix_w_in[..., OFF_QKV:]], axis=-1).astype(BF16)
    w_dt = jnp.repeat(mix_w_in[..., OFF_DT:OFF_QKV], HEAD_DIM, axis=-1).astype(BF16)
    w_out = mix_w_out.astype(BF16)
    lng = sgu_ln_g[:, None, :]
    sgu_b_e = jnp.repeat(jnp.swapaxes(sgu_b, 1, 2), HEAD_DIM, axis=-1)
    per_head = lambda p: jnp.repeat(p, HEAD_DIM, axis=-1)[:, None, :]
    dtb, alog, dskip = per_head(ssm_dt_bias), per_head(ssm_a_log), per_head(ssm_d)
    conv_b = ssm_conv_b[:, None, :]
    norm_g = ssm_norm_g[:, None, :]
    mem_g = mem_norm_g[:, None, :]
    wq, wkv, wxo = xattn_wq.astype(BF16), xattn_wkv.astype(BF16), xattn_wo.astype(BF16)
    biases = _bias_tables(rel_bias)

    h = x.reshape(b * s, d)
    for l in range(depth):
        h = _ffn(h, npre, wi, wo, npost, l, 0)
        a, z, xbc, dt, q, k, v = _inproj(h, npre, w_main, w_dt, lng, sgu_w, sgu_b_e, l)
        seq = lambda t: t.reshape(b, s, t.shape[-1])
        bo = _ssd(seq(z), seq(xbc), seq(dt), ssm_conv_w, conv_b, dtb, alog, dskip, norm_g, l)
        outs, lses = [], []
        for branch in range(len(DILATED_PATTERNS)):
            o, lse = _attn_branch(seq(q), seq(k), seq(v), biases, branch)
            outs.append(o.reshape(b * s, ATTN_WIDTH))
            lses.append(lse.reshape(b * s, ATTN_WIDTH))
        h = _mixout(h, a, bo.reshape(b * s, SSM_INNER), outs, lses, w_out, npost, l)
        mk, mv = _memkv(mem, mem_g, wkv, l)
        h = _xattn(h.reshape(b, s, d), npre, wq, mk, mv, wxo, npost, l).reshape(b * s, d)
        h = _ffn(h, npre, wi, wo, npost, l, 1)
    return h.reshape(b, s, d)
```

```python
import functools
import math

import jax
import jax.numpy as jnp
from jax import lax
from jax.experimental import pallas as pl
from jax.experimental.pallas import tpu as pltpu

F32 = jnp.float32
BF16 = jnp.bfloat16
EPS = 1e-6
LOG2E = math.log2(math.e)
LN2 = math.log(2.0)

LANES = 128
HEAD_DIM = 64
CHUNK = 128
SGU_WIDTH = 256
SGU_GROUPS = SGU_WIDTH // HEAD_DIM
SSM_INNER = 512
SSM_HEADS = SSM_INNER // HEAD_DIM
SSM_GROUPS = 2
SSM_STATE = 128
SSM_CONV = 4
SSM_CONV_CH = SSM_INNER + 2 * SSM_GROUPS * SSM_STATE
GROUP_W = SSM_INNER // SSM_GROUPS
ATTN_WIDTH = 256
ATTN_HEADS = ATTN_WIDTH // HEAD_DIM
DILATED_PATTERNS = ((128, 1), (512, 4), (2048, 16))
REL_BUCKETS = 32
REL_MAX_DIST = 2048
XATTN_HEADS = 4

OFF_Z = 2 * SGU_WIDTH
OFF_XBC = OFF_Z + SSM_INNER
OFF_DT = OFF_XBC + SSM_CONV_CH
OFF_QKV = OFF_DT + SSM_HEADS
W_MAIN = OFF_DT + 3 * ATTN_WIDTH

ROW_TILE = 512
FF_TILE = 256
SSD_TILE = 512
ATTN_TILE = 1024
CONV_PAD = 8
VMEM_LIMIT = 56 * 1024 * 1024


def _params(n_axes):
    return pltpu.CompilerParams(dimension_semantics=("arbitrary",) * n_axes,
                                vmem_limit_bytes=VMEM_LIMIT)


def _rms(x, g):
    return x * lax.rsqrt(jnp.mean(x * x, axis=-1, keepdims=True) + EPS) * g


def _silu(x):
    return x * jax.nn.sigmoid(x)


def _dot(a, b):
    return jnp.dot(a, b, preferred_element_type=F32)


def _dot_nt(a, b):
    return lax.dot_general(a, b, (((1,), (1,)), ((), ())), preferred_element_type=F32)


def _const_spec(shape, index):
    return pl.BlockSpec(shape, lambda *_: index)


def _ffn_body(h_ref, gpre_ref, wi_ref, wo_ref, gpost_ref, o_ref, acc_ref, *, d_ff):
    h = h_ref[...]
    hn = _rms(h, gpre_ref[...]).astype(BF16)
    for c in range(d_ff // FF_TILE):
        lo = c * FF_TILE
        g = _dot(hn, wi_ref[:, lo:lo + FF_TILE])
        u = _dot(hn, wi_ref[:, d_ff + lo:d_ff + lo + FF_TILE])
        y = _dot((_silu(g) * u).astype(BF16), wo_ref[lo:lo + FF_TILE, :])
        if c == 0:
            acc_ref[...] = y
        else:
            acc_ref[...] += y
    o_ref[...] = h + 0.5 * _rms(acc_ref[...], gpost_ref[...])


def _ffn(h, gpre, wi, wo, gpost, layer, which):
    rows, d = h.shape
    d_ff = wo.shape[2]
    return pl.pallas_call(
        functools.partial(_ffn_body, d_ff=d_ff),
        out_shape=jax.ShapeDtypeStruct((rows, d), F32),
        grid=(rows // ROW_TILE,),
        in_specs=[
            pl.BlockSpec((ROW_TILE, d), lambda i: (i, 0)),
            _const_spec((None, None, 1, d), (layer, 3 * which, 0, 0)),
            _const_spec((None, None, d, 2 * d_ff), (layer, which, 0, 0)),
            _const_spec((None, None, d_ff, d), (layer, which, 0, 0)),
            _const_spec((None, None, 1, d), (layer, 3 * which, 0, 0)),
        ],
        out_specs=pl.BlockSpec((ROW_TILE, d), lambda i: (i, 0)),
        scratch_shapes=[pltpu.VMEM((ROW_TILE, d), F32)],
        compiler_params=_params(1),
        name="ffn",
    )(h, gpre, wi, wo, gpost)


def _inproj_body(h_ref, gpre_ref, w_ref, wdt_ref, lng_ref, sw_ref, sb_ref,
                 a_ref, z_ref, xbc_ref, dt_ref, *rest):
    qkv_refs, slab_ref = rest[:-1], rest[-1]
    hn = _rms(h_ref[...], gpre_ref[...]).astype(BF16)

    def proj(lo, hi):
        return _dot(hn, w_ref[:, lo:hi])

    z_ref[...] = proj(OFF_Z, OFF_XBC)
    xbc_ref[...] = proj(OFF_XBC, OFF_DT)
    dt_ref[...] = _dot(hn, wdt_ref[...])

    n_slab = ATTN_WIDTH // LANES
    for t in range(3):
        val = proj(OFF_DT + t * ATTN_WIDTH, OFF_DT + (t + 1) * ATTN_WIDTH)
        if t == 0:
            val = val * (HEAD_DIM ** -0.5 * LOG2E)
        for sl in range(n_slab):
            slab_ref[t * n_slab + sl] = val[:, sl * LANES:(sl + 1) * LANES]
        for bi, (_, dil) in enumerate(DILATED_PATTERNS):
            out = qkv_refs[bi * 3 + t]
            if dil == 1:
                out[...] = val.astype(BF16)
                continue
            for r in range(dil):
                for sl in range(n_slab):
                    lo = r * ATTN_WIDTH + sl * LANES
                    out[:, lo:lo + LANES] = slab_ref[t * n_slab + sl, pl.ds(r, ROW_TILE // dil, stride=dil), :].astype(BF16)

    uv = jax.nn.gelu(proj(0, OFF_Z))
    u = uv[:, :SGU_WIDTH]
    v = uv[:, SGU_WIDTH:]
    mu = jnp.mean(v, axis=-1, keepdims=True)
    var = jnp.mean(jnp.square(v - mu), axis=-1, keepdims=True)
    vn = ((v - mu) * lax.rsqrt(var + EPS) * lng_ref[...]).astype(BF16)
    causal = (lax.broadcasted_iota(jnp.int32, (CHUNK, CHUNK), 0)
              >= lax.broadcasted_iota(jnp.int32, (CHUNK, CHUNK), 1))
    lane_group = lax.broadcasted_iota(jnp.int32, (1, SGU_WIDTH), 1) // HEAD_DIM
    ws = [jnp.where(causal, sw_ref[g], 0.0).astype(BF16) for g in range(SGU_GROUPS)]
    for c in range(ROW_TILE // CHUNK):
        r0 = c * CHUNK
        vc = vn[r0:r0 + CHUNK]
        s = sb_ref[...]
        for g in range(SGU_GROUPS):
            s = s + jnp.where(lane_group == g, _dot(ws[g], vc), 0.0)
        a_ref[r0:r0 + CHUNK, :] = u[r0:r0 + CHUNK] * s


def _inproj(h, gpre, w_main, w_dt, lng, sgu_w, sgu_b, layer):
    rows, d = h.shape
    row_spec = lambda w, dil=1: pl.BlockSpec((ROW_TILE // dil, w * dil), lambda i: (i, 0))
    out_widths = (SGU_WIDTH, SSM_INNER, SSM_CONV_CH, SSM_INNER)
    out_shape = [jax.ShapeDtypeStruct((rows, w), F32) for w in out_widths]
    out_specs = [row_spec(w) for w in out_widths]
    for _, dil in DILATED_PATTERNS:
        out_shape += [jax.ShapeDtypeStruct((rows // dil, ATTN_WIDTH * dil), BF16)] * 3
        out_specs += [row_spec(ATTN_WIDTH, dil)] * 3
    return pl.pallas_call(
        _inproj_body,
        out_shape=out_shape,
        grid=(rows // ROW_TILE,),
        in_specs=[
            row_spec(d),
            _const_spec((None, None, 1, d), (layer, 1, 0, 0)),
            _const_spec((None, d, W_MAIN), (layer, 0, 0)),
            _const_spec((None, d, SSM_INNER), (layer, 0, 0)),
            _const_spec((None, 1, SGU_WIDTH), (layer, 0, 0)),
            _const_spec((None, SGU_GROUPS, CHUNK, CHUNK), (layer, 0, 0, 0)),
            _const_spec((None, CHUNK, SGU_WIDTH), (layer, 0, 0)),
        ],
        out_specs=out_specs,
        scratch_shapes=[pltpu.VMEM((3 * ATTN_WIDTH // LANES, ROW_TILE, LANES), F32)],
        compiler_params=_params(1),
        name="inproj_sgu",
    )(h, gpre, w_main, w_dt, lng, sgu_w, sgu_b)


def _softplus(x):
    return jnp.maximum(x, 0.0) + jnp.log1p(jnp.exp(-jnp.abs(x)))


def _cumsum_rows(tril, a):
    hi = a.astype(BF16)
    r1 = a - hi.astype(F32)
    mid = r1.astype(BF16)
    lo = (r1 - mid.astype(F32)).astype(BF16)
    return _dot(tril, hi) + _dot(tril, mid) + _dot(tril, lo)


def _ssd_body(z_ref, xbc_ref, dt_ref, cw_ref, cb_ref, dtb_ref, alog_ref, d_ref, ng_ref,
              o_ref, ext_ref, hs_ref):
    t = pl.program_id(1)

    @pl.when(t == 0)
    def _():
        ext_ref[0:CONV_PAD, :] = jnp.zeros((CONV_PAD, SSM_CONV_CH), F32)
        hs_ref[...] = jnp.zeros_like(hs_ref)

    @pl.when(t > 0)
    def _():
        ext_ref[0:CONV_PAD, :] = ext_ref[SSD_TILE:SSD_TILE + CONV_PAD, :]

    ext_ref[CONV_PAD:CONV_PAD + SSD_TILE, :] = xbc_ref[0]

    row = lax.broadcasted_iota(jnp.int32, (CHUNK, CHUNK), 0)
    col = lax.broadcasted_iota(jnp.int32, (CHUNK, CHUNK), 1)
    causal = row >= col
    tril = jnp.where(causal, 1.0, 0.0).astype(BF16)
    low_half = col < HEAD_DIM
    lane_head = lax.broadcasted_iota(jnp.int32, (1, GROUP_W), 1) // HEAD_DIM
    a_neg = -jnp.exp(alog_ref[...])

    def conv(r0, lo, hi):
        acc = cb_ref[:, lo:hi]
        for k in range(SSM_CONV):
            start = CONV_PAD + r0 - (SSM_CONV - 1) + k
            acc = acc + ext_ref[start:start + CHUNK, lo:hi] * cw_ref[k:k + 1, lo:hi]
        return _silu(acc)

    for c in range(SSD_TILE // CHUNK):
        r0 = c * CHUNK
        xs = conv(r0, 0, SSM_INNER)
        bm = conv(r0, SSM_INNER, SSM_INNER + SSM_GROUPS * SSM_STATE)
        cm = conv(r0, SSM_INNER + SSM_GROUPS * SSM_STATE, SSM_CONV_CH)
        dt = _softplus(dt_ref[0, r0:r0 + CHUNK, :] + dtb_ref[...])
        acs = _cumsum_rows(tril, dt * a_neg)
        acs_last = acs[CHUNK - 1:CHUNK, :]
        eacs = jnp.exp(acs)
        x = xs * dt
        xb = x.astype(BF16)
        xd = (x * jnp.exp(acs_last - acs)).astype(BF16)
        for g in range(SSM_GROUPS):
            gl = slice(g * GROUP_W, (g + 1) * GROUP_W)
            bg = bm[:, g * SSM_STATE:(g + 1) * SSM_STATE]
            cg = cm[:, g * SSM_STATE:(g + 1) * SSM_STATE].astype(BF16)
            cbm = _dot_nt(cg, bg.astype(BF16))
            xg = xb[:, gl]
            yg = jnp.zeros((CHUNK, GROUP_W), F32)
            for e in range(SSM_HEADS // SSM_GROUPS):
                head = g * (SSM_HEADS // SSM_GROUPS) + e
                pair = acs[:, (head // 2) * 2 * HEAD_DIM:(head // 2 + 1) * 2 * HEAD_DIM]
                swapped = pltpu.roll(pair, HEAD_DIM, 1)
                acs_col = jnp.where(low_half, pair, swapped) if head % 2 == 0 else jnp.where(low_half, swapped, pair)
                seg = acs_col - acs_col.T
                decay = jnp.exp(jnp.where(causal, seg, -jnp.inf))
                yh = _dot((cbm * decay).astype(BF16), xg)
                yg = jnp.where(lane_head == e, yh, yg)
            h_prev = hs_ref[g]
            y_off = _dot(cg, h_prev.astype(BF16)) * eacs[:, gl]
            hs_ref[g] = h_prev * eacs[CHUNK - 1:CHUNK, gl] + _dot(bg.T.astype(BF16), xd[:, gl])
            y = yg + y_off + d_ref[:, gl] * xs[:, gl]
            y = y * _silu(z_ref[0, r0:r0 + CHUNK, gl])
            o_ref[0, r0:r0 + CHUNK, gl] = _rms(y, ng_ref[:, gl])


def _ssd(z, xbc, dt, conv_w, conv_b, dtb, alog, dskip, norm_g, layer):
    b, s, _ = z.shape
    tile = lambda w: pl.BlockSpec((1, SSD_TILE, w), lambda i, j: (i, j, 0))
    vec = lambda w: _const_spec((None, 1, w), (layer, 0, 0))
    return pl.pallas_call(
        _ssd_body,
        out_shape=jax.ShapeDtypeStruct((b, s, SSM_INNER), F32),
        grid=(b, s // SSD_TILE),
        in_specs=[
            tile(SSM_INNER), tile(SSM_CONV_CH), tile(SSM_INNER),
            _const_spec((None, SSM_CONV, SSM_CONV_CH), (layer, 0, 0)),
            vec(SSM_CONV_CH), vec(SSM_INNER), vec(SSM_INNER), vec(SSM_INNER), vec(SSM_INNER),
        ],
        out_specs=tile(SSM_INNER),
        scratch_shapes=[
            pltpu.VMEM((CONV_PAD + SSD_TILE, SSM_CONV_CH), F32),
            pltpu.VMEM((SSM_GROUPS, SSM_STATE, GROUP_W), F32),
        ],
        compiler_params=_params(2),
        name="ssd",
    )(z, xbc, dt, conv_w, conv_b, dtb, alog, dskip, norm_g)


def _attn_body(q_ref, kp_ref, kc_ref, vp_ref, vc_ref, bias_ref, o_ref, lse_ref, *, tq, ncls):
    n = pl.program_id(2)
    nblk = tq // CHUNK
    w = ATTN_WIDTH
    lane_head = lax.broadcasted_iota(jnp.int32, (1, w), 1) // HEAD_DIM
    first_table = jnp.where(n == 0, 1, 0)
    for c in range(ncls):
        cl = slice(c * w, (c + 1) * w)
        for j in range(nblk):
            rows = slice(j * CHUNK, (j + 1) * CHUNK)
            q = q_ref[0, rows, cl]
            lhs = jnp.concatenate([jnp.where(lane_head == h, q, jnp.zeros_like(q)) for h in range(ATTN_HEADS)], axis=0)
            if j == 0:
                kk = jnp.concatenate([kp_ref[0, :, cl], kc_ref[0, 0:CHUNK, cl]], axis=0)
                vv = jnp.concatenate([vp_ref[0, :, cl], vc_ref[0, 0:CHUNK, cl]], axis=0)
            else:
                kk = kc_ref[0, (j - 1) * CHUNK:(j + 1) * CHUNK, cl]
                vv = vc_ref[0, (j - 1) * CHUNK:(j + 1) * CHUNK, cl]
            logits = _dot_nt(lhs, kk) + bias_ref[first_table if j == 0 else 0]
            m = jnp.max(logits, axis=-1, keepdims=True)
            p = jnp.exp2(logits - m)
            s = jnp.sum(p, axis=-1, keepdims=True)
            pv = _dot(p.astype(BF16), vv) * (1.0 / s)
            lse = (m + jnp.log2(s)) * LN2
            o_acc = pv[0:CHUNK]
            lse_acc = jnp.broadcast_to(lse[0:CHUNK], (CHUNK, w))
            for h in range(1, ATTN_HEADS):
                hr = slice(h * CHUNK, (h + 1) * CHUNK)
                o_acc = jnp.where(lane_head == h, pv[hr], o_acc)
                lse_acc = jnp.where(lane_head == h, lse[hr], lse_acc)
            o_ref[0, rows, cl] = o_acc
            lse_ref[0, rows, cl] = lse_acc


def _attn_branch(qr, kr, vr, biases, branch):
    dil = DILATED_PATTERNS[branch][1]
    w = ATTN_WIDTH
    b, cls_len, _ = qr.shape
    tq = min(ATTN_TILE, cls_len)
    ncls = min(dil, ATTN_TILE // tq)
    cur = pl.BlockSpec((1, tq, ncls * w), lambda i, r, n: (i, n, r))
    prev = pl.BlockSpec((1, CHUNK, ncls * w), lambda i, r, n: (i, jnp.maximum(n * (tq // CHUNK) - 1, 0), r))
    o, lse = pl.pallas_call(
        functools.partial(_attn_body, tq=tq, ncls=ncls),
        out_shape=[jax.ShapeDtypeStruct((b, cls_len, dil * w), F32)] * 2,
        grid=(b, dil // ncls, cls_len // tq),
        in_specs=[cur, prev, cur, prev, cur, _const_spec((None, 2, ATTN_HEADS * CHUNK, 2 * CHUNK), (branch, 0, 0, 0))],
        out_specs=[cur, cur],
        compiler_params=_params(3),
        name=f"dilated_attn_{dil}",
    )(qr, kr, kr, vr, vr, biases)
    return o, lse


def _t5_bucket(dist):
    max_exact = REL_BUCKETS // 2
    d = jnp.maximum(dist, 1).astype(F32)
    large = max_exact + (jnp.log(d / max_exact) / math.log(REL_MAX_DIST / max_exact)
                         * (REL_BUCKETS - max_exact)).astype(jnp.int32)
    large = jnp.minimum(large, REL_BUCKETS - 1)
    return jnp.where(dist < max_exact, dist, large)


def _bias_body(rel_ref, span_ref, bucket_ref, o_ref):
    bucket = bucket_ref[0]
    row = lax.broadcasted_iota(jnp.int32, (CHUNK, 2 * CHUNK), 0)
    col = lax.broadcasted_iota(jnp.int32, (CHUNK, 2 * CHUNK), 1)
    dist = row + CHUNK - col
    band = (dist >= 0) & (dist <= span_ref[pl.program_id(0)])
    band_cur = band & (col >= CHUNK)
    for h in range(ATTN_HEADS):
        bias = jnp.zeros((CHUNK, 2 * CHUNK), F32)
        for b in range(REL_BUCKETS):
            bias = jnp.where(bucket == b, rel_ref[b, h], bias)
        bias = bias * LOG2E
        o_ref[0, 0, h * CHUNK:(h + 1) * CHUNK, :] = jnp.where(band, bias, -jnp.inf)
        o_ref[0, 1, h * CHUNK:(h + 1) * CHUNK, :] = jnp.where(band_cur, bias, -jnp.inf)


def _bias_tables(rel_bias):
    assert all(window // dil <= CHUNK for window, dil in DILATED_PATTERNS)
    dist = jnp.arange(CHUNK)[:, None] + CHUNK - jnp.arange(2 * CHUNK)[None, :]
    buckets = jnp.stack([_t5_bucket(jnp.maximum(dist, 0) * dil) for _, dil in DILATED_PATTERNS]).astype(jnp.int32)
    spans = jnp.array([window // dil for window, dil in DILATED_PATTERNS], jnp.int32)
    nb = len(DILATED_PATTERNS)
    return pl.pallas_call(
        _bias_body,
        out_shape=jax.ShapeDtypeStruct((nb, 2, ATTN_HEADS * CHUNK, 2 * CHUNK), F32),
        grid=(nb,),
        in_specs=[pl.BlockSpec(memory_space=pltpu.SMEM), pl.BlockSpec(memory_space=pltpu.SMEM),
                  pl.BlockSpec((1, CHUNK, 2 * CHUNK), lambda i: (i, 0, 0))],
        out_specs=pl.BlockSpec((1, 2, ATTN_HEADS * CHUNK, 2 * CHUNK), lambda i: (i, 0, 0, 0)),
        compiler_params=_params(1),
        name="rel_bias_tables",
    )(rel_bias, spans, buckets)


def _mixout_body(h_ref, a_ref, b_ref, *rest):
    nb = len(DILATED_PATTERNS)
    branch_refs, (w_ref, gpost_ref, out_ref, nat_ref) = rest[:2 * nb], rest[2 * nb:]
    n_slab = ATTN_WIDTH // LANES

    def token_order(idx):
        ref, dil = branch_refs[idx], DILATED_PATTERNS[idx % nb][1]
        if dil == 1:
            return ref[...]
        for r in range(dil):
            for sl in range(n_slab):
                lo = r * ATTN_WIDTH + sl * LANES
                nat_ref[idx * n_slab + sl, pl.ds(r, ROW_TILE // dil, stride=dil), :] = ref[:, lo:lo + LANES]
        return jnp.concatenate([nat_ref[idx * n_slab + sl] for sl in range(n_slab)], axis=-1)

    o1, o2, o3 = (token_order(i) for i in range(nb))
    l1, l2, l3 = (token_order(nb + i) for i in range(nb))
    m = jnp.maximum(jnp.maximum(l1, l2), l3)
    e1, e2, e3 = jnp.exp(l1 - m), jnp.exp(l2 - m), jnp.exp(l3 - m)
    c = (e1 * o1 + e2 * o2 + e3 * o3) * (1.0 / (e1 + e2 + e3))
    off_b = SGU_WIDTH
    off_c = SGU_WIDTH + SSM_INNER
    y = (_dot(a_ref[...].astype(BF16), w_ref[0:off_b, :])
         + _dot(b_ref[...].astype(BF16), w_ref[off_b:off_c, :])
         + _dot(c.astype(BF16), w_ref[off_c:off_c + ATTN_WIDTH, :]))
    out_ref[...] = h_ref[...] + _rms(y, gpost_ref[...])


def _mixout(h, a, bo, outs, lses, w_out, gpost, layer):
    rows, d = h.shape
    row_spec = lambda w, dil=1: pl.BlockSpec((ROW_TILE // dil, w * dil), lambda i: (i, 0))
    attn = [row_spec(ATTN_WIDTH, dil) for _, dil in DILATED_PATTERNS]
    flat = lambda t: t.reshape(-1, t.shape[-1])
    return pl.pallas_call(
        _mixout_body,
        out_shape=jax.ShapeDtypeStruct((rows, d), F32),
        grid=(rows // ROW_TILE,),
        in_specs=[row_spec(d), row_spec(SGU_WIDTH), row_spec(SSM_INNER), *attn, *attn,
                  _const_spec((None, d, d), (layer, 0, 0)),
                  _const_spec((None, None, 1, d), (layer, 1, 0, 0))],
        out_specs=row_spec(d),
        scratch_shapes=[pltpu.VMEM((2 * len(DILATED_PATTERNS) * ATTN_WIDTH // LANES, ROW_TILE, LANES), F32)],
        compiler_params=_params(1),
        name="mix_out",
    )(h, a, bo, *map(flat, outs), *map(flat, lses), w_out, gpost)


def _memkv_body(mem_ref, g_ref, w_ref, k_ref, v_ref):
    d = mem_ref.shape[-1]
    mn = _rms(mem_ref[0], g_ref[...]).astype(BF16)
    k_ref[0] = _dot(mn, w_ref[:, 0:d]).astype(BF16)
    v_ref[0] = _dot(mn, w_ref[:, d:2 * d]).astype(BF16)


def _memkv(mem, g, wkv, layer):
    b, m, d = mem.shape
    blk = pl.BlockSpec((1, m, d), lambda i: (i, 0, 0))
    return pl.pallas_call(
        _memkv_body,
        out_shape=[jax.ShapeDtypeStruct((b, m, d), BF16)] * 2,
        grid=(b,),
        in_specs=[blk, _const_spec((None, 1, d), (layer, 0, 0)), _const_spec((None, d, 2 * d), (layer, 0, 0))],
        out_specs=[blk, blk],
        compiler_params=_params(1),
        name="mem_kv",
    )(mem, g, wkv)


def _xattn_body(h_ref, gpre_ref, wq_ref, k_ref, v_ref, wo_ref, gpost_ref, o_ref):
    h = h_ref[0]
    d = h.shape[-1]
    hd = d // XATTN_HEADS
    hn = _rms(h, gpre_ref[...]).astype(BF16)
    q = (_dot(hn, wq_ref[...]) * (hd ** -0.5)).astype(BF16)
    outs = []
    for i in range(XATTN_HEADS):
        hs = slice(i * hd, (i + 1) * hd)
        logits = _dot_nt(q[:, hs], k_ref[0, :, hs])
        m = jnp.max(logits, axis=-1, keepdims=True)
        p = jnp.exp(logits - m)
        s = jnp.sum(p, axis=-1, keepdims=True)
        outs.append((_dot(p.astype(BF16), v_ref[0, :, hs]) * (1.0 / s)).astype(BF16))
    y = _dot(jnp.concatenate(outs, axis=-1), wo_ref[...])
    o_ref[0] = h + _rms(y, gpost_ref[...])


def _xattn(h, gpre, wq, k, v, wo, gpost, layer):
    b, s, d = h.shape
    m = k.shape[1]
    tile = pl.BlockSpec((1, ROW_TILE, d), lambda i, j: (i, j, 0))
    kv = pl.BlockSpec((1, m, d), lambda i, j: (i, 0, 0))
    return pl.pallas_call(
        _xattn_body,
        out_shape=jax.ShapeDtypeStruct((b, s, d), F32),
        grid=(b, s // ROW_TILE),
        in_specs=[tile, _const_spec((None, None, 1, d), (layer, 2, 0, 0)), _const_spec((None, d, d), (layer, 0, 0)),
                  kv, kv, _const_spec((None, d, d), (layer, 0, 0)), _const_spec((None, None, 1, d), (layer, 2, 0, 0))],
        out_specs=tile,
        compiler_params=_params(2),
        name="mem_xattn",
    )(h, gpre, wq, k, v, wo, gpost)


def kernel(x, mem, norm_pre, norm_post, ffn_wi, ffn_wo, mix_w_in, mix_w_out, sgu_ln_g, sgu_w, sgu_b, ssm_conv_w, ssm_conv_b, ssm_dt_bias, ssm_a_log, ssm_d, ssm_norm_g, rel_bias, mem_norm_g, xattn_wq, xattn_wkv, xattn_wo):
    b, s, d = x.shape
    depth = norm_pre.shape[0]
    assert s % (ATTN_TILE * 1) == 0 and (b * s) % ROW_TILE == 0 and s % SSD_TILE == 0
    assert all(s % (dil * CHUNK) == 0 for _, dil in DILATED_PATTERNS)

    npre = norm_pre[:, :, None, :]
    npost = norm_post[:, :, None, :]
    wi = ffn_wi.astype(BF16)
    wo = ffn_wo.astype(BF16)
    w_main = jnp.concatenate([mix_w_in[..., :OFF_DT], mix_w_in[..., OFF_QKV:]], axis=-1).astype(BF16)
    w_dt = jnp.repeat(mix_w_in[..., OFF_DT:OFF_QKV], HEAD_DIM, axis=-1).astype(BF16)
    w_out = mix_w_out.astype(BF16)
    lng = sgu_ln_g[:, None, :]
    sgu_b_e = jnp.repeat(jnp.swapaxes(sgu_b, 1, 2), HEAD_DIM, axis=-1)
    per_head = lambda p: jnp.repeat(p, HEAD_DIM, axis=-1)[:, None, :]
    dtb, alog, dskip = per_head(ssm_dt_bias), per_head(ssm_a_log), per_head(ssm_d)
    conv_b = ssm_conv_b[:, None, :]
    norm_g = ssm_norm_g[:, None, :]
    mem_g = mem_norm_g[:, None, :]
    wq, wkv, wxo = xattn_wq.astype(BF16), xattn_wkv.astype(BF16), xattn_wo.astype(BF16)
    biases = _bias_tables(rel_bias)

    h = x.reshape(b * s, d)
    for l in range(depth):
        h = _ffn(h, npre, wi, wo, npost, l, 0)
        a, z, xbc, dt, *qkv = _inproj(h, npre, w_main, w_dt, lng, sgu_w, sgu_b_e, l)
        seq = lambda t: t.reshape(b, -1, t.shape[-1])
        bo = _ssd(seq(z), seq(xbc), seq(dt), ssm_conv_w, conv_b, dtb, alog, dskip, norm_g, l)
        outs, lses = [], []
        for branch in range(len(DILATED_PATTERNS)):
            o, lse = _attn_branch(*map(seq, qkv[3 * branch:3 * branch + 3]), biases, branch)
            outs.append(o)
            lses.append(lse)
        h = _mixout(h, a, bo.reshape(b * s, SSM_INNER), outs, lses, w_out, npost, l)
        mk, mv = _memkv(mem, mem_g, wkv, l)
        h = _xattn(h.reshape(b, s, d), npre, wq, mk, mv, wxo, npost, l).reshape(b * s, d)
        h = _ffn(h, npre, wi, wo, npost, l, 1)
    return h.reshape(b, s, d)
```

```python
import functools
import math

import jax
import jax.numpy as jnp
from jax import lax
from jax.experimental import pallas as pl
from jax.experimental.pallas import tpu as pltpu

F32 = jnp.float32
BF16 = jnp.bfloat16
EPS = 1e-6
LOG2E = math.log2(math.e)
LN2 = math.log(2.0)

LANES = 128
MXU_N = 256
HEAD_DIM = 64
CHUNK = 128
SGU_WIDTH = 256
SGU_GROUPS = SGU_WIDTH // HEAD_DIM
SSM_INNER = 512
SSM_HEADS = SSM_INNER // HEAD_DIM
SSM_GROUPS = 2
SSM_STATE = 128
SSM_CONV = 4
SSM_CONV_CH = SSM_INNER + 2 * SSM_GROUPS * SSM_STATE
GROUP_W = SSM_INNER // SSM_GROUPS
ATTN_WIDTH = 256
ATTN_HEADS = ATTN_WIDTH // HEAD_DIM
DILATED_PATTERNS = ((128, 1), (512, 4), (2048, 16))
REL_BUCKETS = 32
REL_MAX_DIST = 2048
XATTN_HEADS = 4

OFF_Z = 2 * SGU_WIDTH
OFF_XBC = OFF_Z + SSM_INNER
OFF_DT = OFF_XBC + SSM_CONV_CH
OFF_QKV = OFF_DT + SSM_HEADS
W_MAIN = OFF_DT + 3 * ATTN_WIDTH

ROW_TILE = 512
FFN_ROW_TILE = 1024
FF_TILE = 256
SSD_TILE = 512
ATTN_TILE = 1024
CONV_PAD = 8
VMEM_LIMIT = 56 * 1024 * 1024


def _params(n_axes):
    return pltpu.CompilerParams(dimension_semantics=("arbitrary",) * n_axes,
                                vmem_limit_bytes=VMEM_LIMIT)


def _rms(x, g):
    return x * lax.rsqrt(jnp.mean(x * x, axis=-1, keepdims=True) + EPS) * g


def _silu(x):
    return x * jax.nn.sigmoid(x)


def _dot(a, b):
    return jnp.dot(a, b, preferred_element_type=F32)


def _dot_nt(a, b):
    return lax.dot_general(a, b, (((1,), (1,)), ((), ())), preferred_element_type=F32)


def _const_spec(shape, index):
    return pl.BlockSpec(shape, lambda *_: index, pipeline_mode=pl.Buffered(1))


def _ffn_body(h_ref, gpre_ref, wi_ref, wo_ref, gpost_ref, o_ref, acc_ref, *, d_ff):
    h = h_ref[...]
    hn = _rms(h, gpre_ref[...]).astype(BF16)
    for c in range(d_ff // FF_TILE):
        lo = c * FF_TILE
        g = _dot(hn, wi_ref[:, lo:lo + FF_TILE])
        u = _dot(hn, wi_ref[:, d_ff + lo:d_ff + lo + FF_TILE])
        y = _dot((_silu(g) * u).astype(BF16), wo_ref[lo:lo + FF_TILE, :])
        if c == 0:
            acc_ref[...] = y
        else:
            acc_ref[...] += y
    o_ref[...] = h + 0.5 * _rms(acc_ref[...], gpost_ref[...])


def _ffn(h, gpre, wi, wo, gpost, layer, which):
    rows, d = h.shape
    d_ff = wo.shape[2]
    return pl.pallas_call(
        functools.partial(_ffn_body, d_ff=d_ff),
        out_shape=jax.ShapeDtypeStruct((rows, d), F32),
        grid=(rows // FFN_ROW_TILE,),
        in_specs=[
            pl.BlockSpec((FFN_ROW_TILE, d), lambda i: (i, 0)),
            _const_spec((None, None, 1, d), (layer, 3 * which, 0, 0)),
            _const_spec((None, None, d, 2 * d_ff), (layer, which, 0, 0)),
            _const_spec((None, None, d_ff, d), (layer, which, 0, 0)),
            _const_spec((None, None, 1, d), (layer, 3 * which, 0, 0)),
        ],
        out_specs=pl.BlockSpec((FFN_ROW_TILE, d), lambda i: (i, 0)),
        scratch_shapes=[pltpu.VMEM((FFN_ROW_TILE, d), F32)],
        compiler_params=_params(1),
        name="ffn",
    )(h, gpre, wi, wo, gpost)


def _inproj_body(h_ref, gpre_ref, w_ref, wdt_ref, lng_ref, sw_ref, sb_ref, cw_ref, cb_ref, dtb_ref,
                 a_ref, z_ref, xs_ref, bc_ref, dt_ref, *rest, tiles_per_seq):
    qkv_refs, (slab_ref, ext_ref) = rest[:-2], rest[-2:]

    first = pl.program_id(0) % tiles_per_seq == 0

    @pl.when(first)
    def _():
        ext_ref[0:CONV_PAD, :] = jnp.zeros((CONV_PAD, SSM_CONV_CH), F32)

    @pl.when(jnp.logical_not(first))
    def _():
        ext_ref[0:CONV_PAD, :] = ext_ref[ROW_TILE:ROW_TILE + CONV_PAD, :]

    hn = _rms(h_ref[...], gpre_ref[...]).astype(BF16)

    def proj(lo, hi):
        return _dot(hn, w_ref[:, lo:hi])

    def stage_xbc(g):
        lo = g * MXU_N
        ext_ref[CONV_PAD:CONV_PAD + ROW_TILE, lo:lo + MXU_N] = proj(OFF_XBC + lo, OFF_XBC + lo + MXU_N)

    def conv_chunk(c, lo, hi, out_ref):
        r0 = c * CHUNK
        win = ext_ref[r0:r0 + CONV_PAD + CHUNK, lo:hi]
        acc = cb_ref[:, lo:hi] + win[CONV_PAD:] * cw_ref[SSM_CONV - 1:SSM_CONV, lo:hi]
        for back in range(1, SSM_CONV):
            tap = SSM_CONV - 1 - back
            acc = acc + pltpu.roll(win, back, 0)[CONV_PAD:] * cw_ref[tap:tap + 1, lo:hi]
        out_ref[r0:r0 + CHUNK, :] = _silu(acc).astype(out_ref.dtype)

    conv_x = lambda c: conv_chunk(c, 0, SSM_INNER, xs_ref)
    conv_bc = lambda c: conv_chunk(c, SSM_INNER, SSM_CONV_CH, bc_ref)

    def dt_half(g):
        ls = slice(g * MXU_N, (g + 1) * MXU_N)
        dt_ref[:, ls] = _softplus(_dot(hn, wdt_ref[:, ls]) + dtb_ref[:, ls])

    uv = proj(0, OFF_Z)
    stage_xbc(0)
    stage_xbc(1)

    uv = jax.nn.gelu(uv)
    u = uv[:, :SGU_WIDTH]
    v = uv[:, SGU_WIDTH:]
    mu = jnp.mean(v, axis=-1, keepdims=True)
    var = jnp.mean(jnp.square(v - mu), axis=-1, keepdims=True)
    vn = ((v - mu) * lax.rsqrt(var + EPS) * lng_ref[...]).astype(BF16)
    causal = (lax.broadcasted_iota(jnp.int32, (CHUNK, CHUNK), 0)
              >= lax.broadcasted_iota(jnp.int32, (CHUNK, CHUNK), 1))
    lane_group = lax.broadcasted_iota(jnp.int32, (1, SGU_WIDTH), 1) // HEAD_DIM
    ws = [jnp.where(causal, sw_ref[g], 0.0).astype(BF16) for g in range(SGU_GROUPS)]
    for c in range(ROW_TILE // CHUNK):
        r0 = c * CHUNK
        vc = vn[r0:r0 + CHUNK]
        s = sb_ref[...]
        for g in range(SGU_GROUPS):
            s = s + jnp.where(lane_group == g, _dot(ws[g], vc), 0.0)
        a_ref[r0:r0 + CHUNK, :] = u[r0:r0 + CHUNK] * s

    stage_xbc(2)
    conv_x(0)
    stage_xbc(3)
    conv_x(1)

    n_slab = ATTN_WIDTH // LANES
    for t, vector_step in enumerate((lambda: conv_x(2), lambda: conv_x(3), lambda: conv_bc(0))):
        val = proj(OFF_DT + t * ATTN_WIDTH, OFF_DT + (t + 1) * ATTN_WIDTH)
        if t == 0:
            val = val * (HEAD_DIM ** -0.5 * LOG2E)
        for sl in range(n_slab):
            slab_ref[t * n_slab + sl] = val[:, sl * LANES:(sl + 1) * LANES]
        for bi, (_, dil) in enumerate(DILATED_PATTERNS):
            out = qkv_refs[bi * 3 + t]
            if dil == 1:
                out[...] = val.astype(BF16)
                continue
            for r in range(dil):
                for sl in range(n_slab):
                    lo = r * ATTN_WIDTH + sl * LANES
                    out[:, lo:lo + LANES] = slab_ref[t * n_slab + sl, pl.ds(r, ROW_TILE // dil, stride=dil), :].astype(BF16)
        vector_step()

    dt_half(0)
    conv_bc(1)
    dt_half(1)
    conv_bc(2)
    z_ref[:, 0:MXU_N] = proj(OFF_Z, OFF_Z + MXU_N)
    conv_bc(3)
    z_ref[:, MXU_N:SSM_INNER] = proj(OFF_Z + MXU_N, OFF_XBC)


def _inproj(h, gpre, w_main, w_dt, lng, sgu_w, sgu_b, conv_w, conv_b, dtb, layer, tiles_per_seq):
    rows, d = h.shape
    row_spec = lambda w, dil=1: pl.BlockSpec((ROW_TILE // dil, w * dil), lambda i: (i, 0))
    vec = lambda w: _const_spec((None, 1, w), (layer, 0, 0))
    outs = ((SGU_WIDTH, F32), (SSM_INNER, F32), (SSM_INNER, F32), (SSM_CONV_CH - SSM_INNER, BF16), (SSM_INNER, F32))
    out_shape = [jax.ShapeDtypeStruct((rows, w), t) for w, t in outs]
    out_specs = [row_spec(w) for w, _ in outs]
    for _, dil in DILATED_PATTERNS:
        out_shape += [jax.ShapeDtypeStruct((rows // dil, ATTN_WIDTH * dil), BF16)] * 3
        out_specs += [row_spec(ATTN_WIDTH, dil)] * 3
    return pl.pallas_call(
        functools.partial(_inproj_body, tiles_per_seq=tiles_per_seq),
        out_shape=out_shape,
        grid=(rows // ROW_TILE,),
        in_specs=[
            row_spec(d),
            _const_spec((None, None, 1, d), (layer, 1, 0, 0)),
            _const_spec((None, d, W_MAIN), (layer, 0, 0)),
            _const_spec((None, d, SSM_INNER), (layer, 0, 0)),
            vec(SGU_WIDTH),
            _const_spec((None, SGU_GROUPS, CHUNK, CHUNK), (layer, 0, 0, 0)),
            _const_spec((None, CHUNK, SGU_WIDTH), (layer, 0, 0)),
            _const_spec((None, SSM_CONV, SSM_CONV_CH), (layer, 0, 0)),
            vec(SSM_CONV_CH), vec(SSM_INNER),
        ],
        out_specs=out_specs,
        scratch_shapes=[pltpu.VMEM((3 * ATTN_WIDTH // LANES, ROW_TILE, LANES), F32),
                        pltpu.VMEM((CONV_PAD + ROW_TILE, SSM_CONV_CH), F32)],
        compiler_params=_params(1),
        name="inproj_sgu",
    )(h, gpre, w_main, w_dt, lng, sgu_w, sgu_b, conv_w, conv_b, dtb)


def _softplus(x):
    return jnp.maximum(x, 0.0) + jnp.log1p(jnp.exp(-jnp.abs(x)))


def _cumsum_rows(tril, a):
    hi = a.astype(BF16)
    r1 = a - hi.astype(F32)
    mid = r1.astype(BF16)
    lo = (r1 - mid.astype(F32)).astype(BF16)
    return _dot(tril, hi) + _dot(tril, mid) + _dot(tril, lo)


def _ssd_body(z_ref, xs_ref, bc_ref, dt_ref, alog_ref, d_ref, ng_ref, o_ref, hs_ref):
    @pl.when(pl.program_id(1) == 0)
    def _():
        hs_ref[...] = jnp.zeros_like(hs_ref)

    row = lax.broadcasted_iota(jnp.int32, (CHUNK, CHUNK), 0)
    col = lax.broadcasted_iota(jnp.int32, (CHUNK, CHUNK), 1)
    causal = row >= col
    tril = jnp.where(causal, 1.0, 0.0).astype(BF16)
    low_half = col < HEAD_DIM
    lane_head = lax.broadcasted_iota(jnp.int32, (1, GROUP_W), 1) // HEAD_DIM
    a_neg = -jnp.exp(alog_ref[...])
    c_off = SSM_GROUPS * SSM_STATE

    for c in range(SSD_TILE // CHUNK):
        r0 = c * CHUNK
        xs = xs_ref[0, r0:r0 + CHUNK, :]
        dt = dt_ref[0, r0:r0 + CHUNK, :]
        acs = _cumsum_rows(tril, dt * a_neg)
        acs_last = acs[CHUNK - 1:CHUNK, :]
        eacs = jnp.exp(acs)
        x = xs * dt
        xb = x.astype(BF16)
        xd = (x * jnp.exp(acs_last - acs)).astype(BF16)
        for g in range(SSM_GROUPS):
            gl = slice(g * GROUP_W, (g + 1) * GROUP_W)
            bg = bc_ref[0, r0:r0 + CHUNK, g * SSM_STATE:(g + 1) * SSM_STATE]
            cg = bc_ref[0, r0:r0 + CHUNK, c_off + g * SSM_STATE:c_off + (g + 1) * SSM_STATE]
            cbm = _dot_nt(cg, bg)
            xg = xb[:, gl]
            yg = jnp.zeros((CHUNK, GROUP_W), F32)
            for e in range(SSM_HEADS // SSM_GROUPS):
                head = g * (SSM_HEADS // SSM_GROUPS) + e
                pair = acs[:, (head // 2) * 2 * HEAD_DIM:(head // 2 + 1) * 2 * HEAD_DIM]
                swapped = pltpu.roll(pair, HEAD_DIM, 1)
                acs_col = jnp.where(low_half, pair, swapped) if head % 2 == 0 else jnp.where(low_half, swapped, pair)
                seg = acs_col - acs_col.T
                decay = jnp.exp(jnp.where(causal, seg, -jnp.inf))
                yh = _dot((cbm * decay).astype(BF16), xg)
                yg = jnp.where(lane_head == e, yh, yg)
            h_prev = hs_ref[g]
            y_off = _dot(cg, h_prev.astype(BF16)) * eacs[:, gl]
            hs_ref[g] = h_prev * eacs[CHUNK - 1:CHUNK, gl] + _dot(bg.astype(F32).T.astype(BF16), xd[:, gl])
            y = yg + y_off + d_ref[:, gl] * xs[:, gl]
            y = y * _silu(z_ref[0, r0:r0 + CHUNK, gl])
            o_ref[0, r0:r0 + CHUNK, gl] = _rms(y, ng_ref[:, gl])


def _ssd(z, xs, bc, dt, alog, dskip, norm_g, layer):
    b, s, _ = z.shape
    tile = lambda w: pl.BlockSpec((1, SSD_TILE, w), lambda i, j: (i, j, 0))
    vec = lambda w: _const_spec((None, 1, w), (layer, 0, 0))
    return pl.pallas_call(
        _ssd_body,
        out_shape=jax.ShapeDtypeStruct((b, s, SSM_INNER), F32),
        grid=(b, s // SSD_TILE),
        in_specs=[tile(SSM_INNER), tile(SSM_INNER), tile(SSM_CONV_CH - SSM_INNER), tile(SSM_INNER),
                  vec(SSM_INNER), vec(SSM_INNER), vec(SSM_INNER)],
        out_specs=tile(SSM_INNER),
        scratch_shapes=[pltpu.VMEM((SSM_GROUPS, SSM_STATE, GROUP_W), F32)],
        compiler_params=_params(2),
        name="ssd",
    )(z, xs, bc, dt, alog, dskip, norm_g)


def _attn_body(q_ref, kp_ref, kc_ref, vp_ref, vc_ref, bias_ref, o_ref, lse_ref, *, tq, ncls):
    n = pl.program_id(2)
    nblk = tq // CHUNK
    w = ATTN_WIDTH
    lane_head = lax.broadcasted_iota(jnp.int32, (1, w), 1) // HEAD_DIM
    first_table = jnp.where(n == 0, 1, 0)
    for c in range(ncls):
        cl = slice(c * w, (c + 1) * w)
        for j in range(nblk):
            rows = slice(j * CHUNK, (j + 1) * CHUNK)
            q = q_ref[0, rows, cl]
            lhs = jnp.concatenate([jnp.where(lane_head == h, q, jnp.zeros_like(q)) for h in range(ATTN_HEADS)], axis=0)
            if j == 0:
                kk = jnp.concatenate([kp_ref[0, :, cl], kc_ref[0, 0:CHUNK, cl]], axis=0)
                vv = jnp.concatenate([vp_ref[0, :, cl], vc_ref[0, 0:CHUNK, cl]], axis=0)
            else:
                kk = kc_ref[0, (j - 1) * CHUNK:(j + 1) * CHUNK, cl]
                vv = vc_ref[0, (j - 1) * CHUNK:(j + 1) * CHUNK, cl]
            logits = _dot_nt(lhs, kk) + bias_ref[first_table if j == 0 else 0]
            m = jnp.max(logits, axis=-1, keepdims=True)
            p = jnp.exp2(logits - m)
            s = jnp.sum(p, axis=-1, keepdims=True)
            pv = _dot(p.astype(BF16), vv) * (1.0 / s)
            lse = (m + jnp.log2(s)) * LN2
            o_acc = pv[0:CHUNK]
            lse_acc = jnp.broadcast_to(lse[0:CHUNK], (CHUNK, w))
            for h in range(1, ATTN_HEADS):
                hr = slice(h * CHUNK, (h + 1) * CHUNK)
                o_acc = jnp.where(lane_head == h, pv[hr], o_acc)
                lse_acc = jnp.where(lane_head == h, lse[hr], lse_acc)
            o_ref[0, rows, cl] = o_acc
            lse_ref[0, rows, cl] = lse_acc


def _attn_branch(qr, kr, vr, biases, branch):
    dil = DILATED_PATTERNS[branch][1]
    w = ATTN_WIDTH
    b, cls_len, _ = qr.shape
    tq = min(ATTN_TILE, cls_len)
    ncls = min(dil, ATTN_TILE // tq)
    cur = pl.BlockSpec((1, tq, ncls * w), lambda i, r, n: (i, n, r))
    prev = pl.BlockSpec((1, CHUNK, ncls * w), lambda i, r, n: (i, jnp.maximum(n * (tq // CHUNK) - 1, 0), r))
    o, lse = pl.pallas_call(
        functools.partial(_attn_body, tq=tq, ncls=ncls),
        out_shape=[jax.ShapeDtypeStruct((b, cls_len, dil * w), F32)] * 2,
        grid=(b, dil // ncls, cls_len // tq),
        in_specs=[cur, prev, cur, prev, cur, _const_spec((None, 2, ATTN_HEADS * CHUNK, 2 * CHUNK), (branch, 0, 0, 0))],
        out_specs=[cur, cur],
        compiler_params=_params(3),
        name=f"dilated_attn_{dil}",
    )(qr, kr, kr, vr, vr, biases)
    return o, lse


def _t5_bucket(dist):
    max_exact = REL_BUCKETS // 2
    d = jnp.maximum(dist, 1).astype(F32)
    large = max_exact + (jnp.log(d / max_exact) / math.log(REL_MAX_DIST / max_exact)
                         * (REL_BUCKETS - max_exact)).astype(jnp.int32)
    large = jnp.minimum(large, REL_BUCKETS - 1)
    return jnp.where(dist < max_exact, dist, large)


def _bias_body(rel_ref, span_ref, bucket_ref, o_ref):
    bucket = bucket_ref[0]
    row = lax.broadcasted_iota(jnp.int32, (CHUNK, 2 * CHUNK), 0)
    col = lax.broadcasted_iota(jnp.int32, (CHUNK, 2 * CHUNK), 1)
    dist = row + CHUNK - col
    band = (dist >= 0) & (dist <= span_ref[pl.program_id(0)])
    band_cur = band & (col >= CHUNK)
    for h in range(ATTN_HEADS):
        bias = jnp.zeros((CHUNK, 2 * CHUNK), F32)
        for b in range(REL_BUCKETS):
            bias = jnp.where(bucket == b, rel_ref[b, h], bias)
        bias = bias * LOG2E
        o_ref[0, 0, h * CHUNK:(h + 1) * CHUNK, :] = jnp.where(band, bias, -jnp.inf)
        o_ref[0, 1, h * CHUNK:(h + 1) * CHUNK, :] = jnp.where(band_cur, bias, -jnp.inf)


def _bias_tables(rel_bias):
    assert all(window // dil <= CHUNK for window, dil in DILATED_PATTERNS)
    dist = jnp.arange(CHUNK)[:, None] + CHUNK - jnp.arange(2 * CHUNK)[None, :]
    buckets = jnp.stack([_t5_bucket(jnp.maximum(dist, 0) * dil) for _, dil in DILATED_PATTERNS]).astype(jnp.int32)
    spans = jnp.array([window // dil for window, dil in DILATED_PATTERNS], jnp.int32)
    nb = len(DILATED_PATTERNS)
    return pl.pallas_call(
        _bias_body,
        out_shape=jax.ShapeDtypeStruct((nb, 2, ATTN_HEADS * CHUNK, 2 * CHUNK), F32),
        grid=(nb,),
        in_specs=[pl.BlockSpec(memory_space=pltpu.SMEM), pl.BlockSpec(memory_space=pltpu.SMEM),
                  pl.BlockSpec((1, CHUNK, 2 * CHUNK), lambda i: (i, 0, 0))],
        out_specs=pl.BlockSpec((1, 2, ATTN_HEADS * CHUNK, 2 * CHUNK), lambda i: (i, 0, 0, 0)),
        compiler_params=_params(1),
        name="rel_bias_tables",
    )(rel_bias, spans, buckets)


def _mixout_body(h_ref, a_ref, b_ref, *rest):
    nb = len(DILATED_PATTERNS)
    branch_refs, (w_ref, gpost_ref, out_ref, nat_ref) = rest[:2 * nb], rest[2 * nb:]
    n_slab = ATTN_WIDTH // LANES

    def token_order(idx):
        ref, dil = branch_refs[idx], DILATED_PATTERNS[idx % nb][1]
        if dil == 1:
            return ref[...]
        for r in range(dil):
            for sl in range(n_slab):
                lo = r * ATTN_WIDTH + sl * LANES
                nat_ref[idx * n_slab + sl, pl.ds(r, ROW_TILE // dil, stride=dil), :] = ref[:, lo:lo + LANES]
        return jnp.concatenate([nat_ref[idx * n_slab + sl] for sl in range(n_slab)], axis=-1)

    o1, o2, o3 = (token_order(i) for i in range(nb))
    l1, l2, l3 = (token_order(nb + i) for i in range(nb))
    m = jnp.maximum(jnp.maximum(l1, l2), l3)
    e1, e2, e3 = jnp.exp(l1 - m), jnp.exp(l2 - m), jnp.exp(l3 - m)
    c = (e1 * o1 + e2 * o2 + e3 * o3) * (1.0 / (e1 + e2 + e3))
    off_b = SGU_WIDTH
    off_c = SGU_WIDTH + SSM_INNER
    y = (_dot(a_ref[...].astype(BF16), w_ref[0:off_b, :])
         + _dot(b_ref[...].astype(BF16), w_ref[off_b:off_c, :])
         + _dot(c.astype(BF16), w_ref[off_c:off_c + ATTN_WIDTH, :]))
    out_ref[...] = h_ref[...] + _rms(y, gpost_ref[...])


def _mixout(h, a, bo, outs, lses, w_out, gpost, layer):
    rows, d = h.shape
    row_spec = lambda w, dil=1: pl.BlockSpec((ROW_TILE // dil, w * dil), lambda i: (i, 0))
    attn = [row_spec(ATTN_WIDTH, dil) for _, dil in DILATED_PATTERNS]
    flat = lambda t: t.reshape(-1, t.shape[-1])
    return pl.pallas_call(
        _mixout_body,
        out_shape=jax.ShapeDtypeStruct((rows, d), F32),
        grid=(rows // ROW_TILE,),
        in_specs=[row_spec(d), row_spec(SGU_WIDTH), row_spec(SSM_INNER), *attn, *attn,
                  _const_spec((None, d, d), (layer, 0, 0)),
                  _const_spec((None, None, 1, d), (layer, 1, 0, 0))],
        out_specs=row_spec(d),
        scratch_shapes=[pltpu.VMEM((2 * len(DILATED_PATTERNS) * ATTN_WIDTH // LANES, ROW_TILE, LANES), F32)],
        compiler_params=_params(1),
        name="mix_out",
    )(h, a, bo, *map(flat, outs), *map(flat, lses), w_out, gpost)


def _memkv_body(mem_ref, g_ref, w_ref, k_ref, v_ref):
    d = mem_ref.shape[-1]
    mn = _rms(mem_ref[0], g_ref[...]).astype(BF16)
    k_ref[0] = _dot(mn, w_ref[:, 0:d]).astype(BF16)
    v_ref[0] = _dot(mn, w_ref[:, d:2 * d]).astype(BF16)


def _memkv(mem, g, wkv, layer):
    b, m, d = mem.shape
    blk = pl.BlockSpec((1, m, d), lambda i: (i, 0, 0))
    return pl.pallas_call(
        _memkv_body,
        out_shape=[jax.ShapeDtypeStruct((b, m, d), BF16)] * 2,
        grid=(b,),
        in_specs=[blk, _const_spec((None, 1, d), (layer, 0, 0)), _const_spec((None, d, 2 * d), (layer, 0, 0))],
        out_specs=[blk, blk],
        compiler_params=_params(1),
        name="mem_kv",
    )(mem, g, wkv)


def _xattn_body(h_ref, gpre_ref, wq_ref, k_ref, v_ref, wo_ref, gpost_ref, o_ref):
    h = h_ref[0]
    d = h.shape[-1]
    hd = d // XATTN_HEADS
    hn = _rms(h, gpre_ref[...]).astype(BF16)
    q = (_dot(hn, wq_ref[...]) * (hd ** -0.5)).astype(BF16)
    outs = []
    for i in range(XATTN_HEADS):
        hs = slice(i * hd, (i + 1) * hd)
        logits = _dot_nt(q[:, hs], k_ref[0, :, hs])
        m = jnp.max(logits, axis=-1, keepdims=True)
        p = jnp.exp(logits - m)
        s = jnp.sum(p, axis=-1, keepdims=True)
        outs.append((_dot(p.astype(BF16), v_ref[0, :, hs]) * (1.0 / s)).astype(BF16))
    y = _dot(jnp.concatenate(outs, axis=-1), wo_ref[...])
    o_ref[0] = h + _rms(y, gpost_ref[...])


def _xattn(h, gpre, wq, k, v, wo, gpost, layer):
    b, s, d = h.shape
    m = k.shape[1]
    tile = pl.BlockSpec((1, ROW_TILE, d), lambda i, j: (i, j, 0))
    kv = pl.BlockSpec((1, m, d), lambda i, j: (i, 0, 0))
    return pl.pallas_call(
        _xattn_body,
        out_shape=jax.ShapeDtypeStruct((b, s, d), F32),
        grid=(b, s // ROW_TILE),
        in_specs=[tile, _const_spec((None, None, 1, d), (layer, 2, 0, 0)), _const_spec((None, d, d), (layer, 0, 0)),
                  kv, kv, _const_spec((None, d, d), (layer, 0, 0)), _const_spec((None, None, 1, d), (layer, 2, 0, 0))],
        out_specs=tile,
        compiler_params=_params(2),
        name="mem_xattn",
    )(h, gpre, wq, k, v, wo, gpost)


def kernel(x, mem, norm_pre, norm_post, ffn_wi, ffn_wo, mix_w_in, mix_w_out, sgu_ln_g, sgu_w, sgu_b, ssm_conv_w, ssm_conv_b, ssm_dt_bias, ssm_a_log, ssm_d, ssm_norm_g, rel_bias, mem_norm_g, xattn_wq, xattn_wkv, xattn_wo):
    b, s, d = x.shape
    depth = norm_pre.shape[0]
    assert s % ATTN_TILE == 0 and s % ROW_TILE == 0 and s % SSD_TILE == 0
    assert all(s % (dil * CHUNK) == 0 for _, dil in DILATED_PATTERNS)

    npre = norm_pre[:, :, None, :]
    npost = norm_post[:, :, None, :]
    wi = ffn_wi.astype(BF16)
    wo = ffn_wo.astype(BF16)
    w_main = jnp.concatenate([mix_w_in[..., :OFF_DT], mix_w_in[..., OFF_QKV:]], axis=-1).astype(BF16)
    w_dt = jnp.repeat(mix_w_in[..., OFF_DT:OFF_QKV], HEAD_DIM, axis=-1).astype(BF16)
    w_out = mix_w_out.astype(BF16)
    lng = sgu_ln_g[:, None, :]
    sgu_b_e = jnp.repeat(jnp.swapaxes(sgu_b, 1, 2), HEAD_DIM, axis=-1)
    per_head = lambda p: jnp.repeat(p, HEAD_DIM, axis=-1)[:, None, :]
    dtb, alog, dskip = per_head(ssm_dt_bias), per_head(ssm_a_log), per_head(ssm_d)
    conv_b = ssm_conv_b[:, None, :]
    norm_g = ssm_norm_g[:, None, :]
    mem_g = mem_norm_g[:, None, :]
    wq, wkv, wxo = xattn_wq.astype(BF16), xattn_wkv.astype(BF16), xattn_wo.astype(BF16)
    biases = _bias_tables(rel_bias)

    h = x.reshape(b * s, d)
    for l in range(depth):
        h = _ffn(h, npre, wi, wo, npost, l, 0)
        a, z, xs, bc, dt, *qkv = _inproj(h, npre, w_main, w_dt, lng, sgu_w, sgu_b_e, ssm_conv_w, conv_b, dtb, l,
                                         s // ROW_TILE)
        seq = lambda t: t.reshape(b, -1, t.shape[-1])
        bo = _ssd(seq(z), seq(xs), seq(bc), seq(dt), alog, dskip, norm_g, l)
        outs, lses = [], []
        for branch in range(len(DILATED_PATTERNS)):
            o, lse = _attn_branch(*map(seq, qkv[3 * branch:3 * branch + 3]), biases, branch)
            outs.append(o)
            lses.append(lse)
        h = _mixout(h, a, bo.reshape(b * s, SSM_INNER), outs, lses, w_out, npost, l)
        mk, mv = _memkv(mem, mem_g, wkv, l)
        h = _xattn(h.reshape(b, s, d), npre, wq, mk, mv, wxo, npost, l).reshape(b * s, d)
        h = _ffn(h, npre, wi, wo, npost, l, 1)
    return h.reshape(b, s, d)
```

```python
import functools
import math

import jax
import jax.numpy as jnp
from jax import lax
from jax.experimental import pallas as pl
from jax.experimental.pallas import tpu as pltpu

F32 = jnp.float32
BF16 = jnp.bfloat16
EPS = 1e-6
LOG2E = math.log2(math.e)

LANES = 128
MXU_N = 256
HEAD_DIM = 64
CHUNK = 128
SGU_WIDTH = 256
SGU_GROUPS = SGU_WIDTH // HEAD_DIM
SSM_INNER = 512
SSM_HEADS = SSM_INNER // HEAD_DIM
SSM_GROUPS = 2
SSM_STATE = 128
SSM_CONV = 4
SSM_CONV_CH = SSM_INNER + 2 * SSM_GROUPS * SSM_STATE
GROUP_W = SSM_INNER // SSM_GROUPS
ATTN_WIDTH = 256
ATTN_HEADS = ATTN_WIDTH // HEAD_DIM
DILATED_PATTERNS = ((128, 1), (512, 4), (2048, 16))
REL_BUCKETS = 32
REL_MAX_DIST = 2048
XATTN_HEADS = 4

OFF_Z = 2 * SGU_WIDTH
OFF_XBC = OFF_Z + SSM_INNER
OFF_DT = OFF_XBC + SSM_CONV_CH
OFF_QKV = OFF_DT + SSM_HEADS
W_MAIN = OFF_DT + 3 * ATTN_WIDTH

ROW_TILE = 512
FFN_ROW_TILE = 1024
MIX_ROW_TILE = 512
MIX_HALF = MIX_ROW_TILE // 2
FF_TILE = 256
SSD_TILE = 1024
ATTN_TILE = 2048
CONV_PAD = 8
VMEM_LIMIT = 56 * 1024 * 1024


def _params(n_axes):
    return pltpu.CompilerParams(dimension_semantics=("arbitrary",) * n_axes,
                                vmem_limit_bytes=VMEM_LIMIT)


def _rms(x, g):
    return x * lax.rsqrt(jnp.mean(x * x, axis=-1, keepdims=True) + EPS) * g


def _silu(x):
    return x * jax.nn.sigmoid(x)


def _dot(a, b):
    return jnp.dot(a, b, preferred_element_type=F32)


def _dot_nt(a, b):
    return lax.dot_general(a, b, (((1,), (1,)), ((), ())), preferred_element_type=F32)


def _const_spec(shape, index):
    return pl.BlockSpec(shape, lambda *_: index, pipeline_mode=pl.Buffered(1))


def _ffn_body(h_ref, gpre_ref, wi_ref, wo_ref, gpost_ref, o_ref, acc_ref, *, d_ff):
    for r0 in range(0, FFN_ROW_TILE, ROW_TILE):
        rows = slice(r0, r0 + ROW_TILE)
        h = h_ref[rows, :]
        hn = _rms(h, gpre_ref[...]).astype(BF16)
        for c in range(d_ff // FF_TILE):
            lo = c * FF_TILE
            g = _dot(hn, wi_ref[:, lo:lo + FF_TILE])
            u = _dot(hn, wi_ref[:, d_ff + lo:d_ff + lo + FF_TILE])
            y = _dot((_silu(g) * u).astype(BF16), wo_ref[lo:lo + FF_TILE, :])
            if c == 0:
                acc_ref[rows, :] = y
            else:
                acc_ref[rows, :] += y
        o_ref[rows, :] = h + 0.5 * _rms(acc_ref[rows, :], gpost_ref[...])


def _ffn(h, gpre, wi, wo, gpost, layer, which):
    rows, d = h.shape
    d_ff = wo.shape[2]
    return pl.pallas_call(
        functools.partial(_ffn_body, d_ff=d_ff),
        out_shape=jax.ShapeDtypeStruct((rows, d), F32),
        grid=(rows // FFN_ROW_TILE,),
        in_specs=[
            pl.BlockSpec((FFN_ROW_TILE, d), lambda i: (i, 0)),
            _const_spec((None, None, 1, d), (layer, 3 * which, 0, 0)),
            _const_spec((None, None, d, 2 * d_ff), (layer, which, 0, 0)),
            _const_spec((None, None, d_ff, d), (layer, which, 0, 0)),
            _const_spec((None, None, 1, d), (layer, 3 * which, 0, 0)),
        ],
        out_specs=pl.BlockSpec((FFN_ROW_TILE, d), lambda i: (i, 0)),
        scratch_shapes=[pltpu.VMEM((FFN_ROW_TILE, d), F32)],
        compiler_params=_params(1),
        name="ffn",
    )(h, gpre, wi, wo, gpost)


def _inproj_body(h_ref, gpre_ref, w_ref, wdt_ref, lng_ref, sw_ref, sb_ref, cw_ref, cb_ref, dtb_ref,
                 a_ref, z_ref, xs_ref, bc_ref, dt_ref, *rest, tiles_per_seq):
    qkv_refs, (slab_ref, ext_ref) = rest[:-2], rest[-2:]

    first = pl.program_id(0) % tiles_per_seq == 0

    @pl.when(first)
    def _():
        ext_ref[0:CONV_PAD, :] = jnp.zeros((CONV_PAD, SSM_CONV_CH), F32)

    @pl.when(jnp.logical_not(first))
    def _():
        ext_ref[0:CONV_PAD, :] = ext_ref[ROW_TILE:ROW_TILE + CONV_PAD, :]

    hn = _rms(h_ref[...], gpre_ref[...]).astype(BF16)

    def proj(lo, hi):
        return _dot(hn, w_ref[:, lo:hi])

    def stage_xbc(g):
        lo = g * MXU_N
        ext_ref[CONV_PAD:CONV_PAD + ROW_TILE, lo:lo + MXU_N] = proj(OFF_XBC + lo, OFF_XBC + lo + MXU_N)

    def conv_chunk(c, lo, hi, out_ref):
        r0 = c * CHUNK
        for l0 in range(lo, hi, LANES):
            ls = slice(l0, l0 + LANES)
            win = ext_ref[r0:r0 + CONV_PAD + CHUNK, ls]
            acc = cb_ref[:, ls] + win[CONV_PAD:] * cw_ref[SSM_CONV - 1:SSM_CONV, ls]
            for back in range(1, SSM_CONV):
                tap = SSM_CONV - 1 - back
                acc = acc + pltpu.roll(win, back, 0)[CONV_PAD:] * cw_ref[tap:tap + 1, ls]
            out_ref[r0:r0 + CHUNK, l0 - lo:l0 - lo + LANES] = _silu(acc).astype(out_ref.dtype)

    conv_x = lambda c: conv_chunk(c, 0, SSM_INNER, xs_ref)
    conv_bc = lambda c: conv_chunk(c, SSM_INNER, SSM_CONV_CH, bc_ref)

    def dt_half(g):
        ls = slice(g * MXU_N, (g + 1) * MXU_N)
        dt_ref[:, ls] = _softplus(_dot(hn, wdt_ref[:, ls]) + dtb_ref[:, ls])

    uv = proj(0, OFF_Z)
    stage_xbc(0)
    stage_xbc(1)

    uv = jax.nn.gelu(uv)
    u = uv[:, :SGU_WIDTH]
    v = uv[:, SGU_WIDTH:]
    mu = jnp.mean(v, axis=-1, keepdims=True)
    var = jnp.mean(jnp.square(v - mu), axis=-1, keepdims=True)
    vn = ((v - mu) * lax.rsqrt(var + EPS) * lng_ref[...]).astype(BF16)
    causal = (lax.broadcasted_iota(jnp.int32, (CHUNK, CHUNK), 0)
              >= lax.broadcasted_iota(jnp.int32, (CHUNK, CHUNK), 1))
    lane_group = lax.broadcasted_iota(jnp.int32, (1, SGU_WIDTH), 1) // HEAD_DIM
    w_cat = jnp.concatenate([jnp.where(causal, sw_ref[g], 0.0).astype(BF16) for g in range(SGU_GROUPS)], axis=1)
    for c in range(ROW_TILE // CHUNK):
        r0 = c * CHUNK
        vc = vn[r0:r0 + CHUNK]
        v_blocks = [jnp.where(lane_group == g, vc, jnp.zeros_like(vc)) for g in range(SGU_GROUPS)]
        s = sb_ref[...] + _dot(w_cat, jnp.concatenate(v_blocks, axis=0))
        a_ref[r0:r0 + CHUNK, :] = u[r0:r0 + CHUNK] * s

    stage_xbc(2)
    conv_x(0)
    stage_xbc(3)
    conv_x(1)

    n_slab = ATTN_WIDTH // LANES
    for t, vector_step in enumerate((lambda: conv_x(2), lambda: conv_x(3), lambda: conv_bc(0))):
        val = proj(OFF_DT + t * ATTN_WIDTH, OFF_DT + (t + 1) * ATTN_WIDTH)
        if t == 0:
            val = val * (HEAD_DIM ** -0.5 * LOG2E)
        for sl in range(n_slab):
            slab_ref[t * n_slab + sl] = val[:, sl * LANES:(sl + 1) * LANES]
        for bi, (_, dil) in enumerate(DILATED_PATTERNS):
            out = qkv_refs[bi * 3 + t]
            if dil == 1:
                out[...] = val.astype(BF16)
                continue
            for r in range(dil):
                for sl in range(n_slab):
                    lo = r * ATTN_WIDTH + sl * LANES
                    out[:, lo:lo + LANES] = slab_ref[t * n_slab + sl, pl.ds(r, ROW_TILE // dil, stride=dil), :].astype(BF16)
        vector_step()

    dt_half(0)
    conv_bc(1)
    dt_half(1)
    conv_bc(2)
    z_ref[:, 0:MXU_N] = proj(OFF_Z, OFF_Z + MXU_N)
    conv_bc(3)
    z_ref[:, MXU_N:SSM_INNER] = proj(OFF_Z + MXU_N, OFF_XBC)


def _inproj(h, gpre, w_main, w_dt, lng, sgu_w, sgu_b, conv_w, conv_b, dtb, layer, tiles_per_seq):
    rows, d = h.shape
    row_spec = lambda w, dil=1: pl.BlockSpec((ROW_TILE // dil, w * dil), lambda i: (i, 0))
    vec = lambda w: _const_spec((None, 1, w), (layer, 0, 0))
    outs = ((SGU_WIDTH, F32), (SSM_INNER, F32), (SSM_INNER, F32), (SSM_CONV_CH - SSM_INNER, BF16), (SSM_INNER, F32))
    out_shape = [jax.ShapeDtypeStruct((rows, w), t) for w, t in outs]
    out_specs = [row_spec(w) for w, _ in outs]
    for _, dil in DILATED_PATTERNS:
        out_shape += [jax.ShapeDtypeStruct((rows // dil, ATTN_WIDTH * dil), BF16)] * 3
        out_specs += [row_spec(ATTN_WIDTH, dil)] * 3
    return pl.pallas_call(
        functools.partial(_inproj_body, tiles_per_seq=tiles_per_seq),
        out_shape=out_shape,
        grid=(rows // ROW_TILE,),
        in_specs=[
            row_spec(d),
            _const_spec((None, None, 1, d), (layer, 1, 0, 0)),
            _const_spec((None, d, W_MAIN), (layer, 0, 0)),
            _const_spec((None, d, SSM_INNER), (layer, 0, 0)),
            vec(SGU_WIDTH),
            _const_spec((None, SGU_GROUPS, CHUNK, CHUNK), (layer, 0, 0, 0)),
            _const_spec((None, CHUNK, SGU_WIDTH), (layer, 0, 0)),
            _const_spec((None, SSM_CONV, SSM_CONV_CH), (layer, 0, 0)),
            vec(SSM_CONV_CH), vec(SSM_INNER),
        ],
        out_specs=out_specs,
        scratch_shapes=[pltpu.VMEM((3 * ATTN_WIDTH // LANES, ROW_TILE, LANES), F32),
                        pltpu.VMEM((CONV_PAD + ROW_TILE, SSM_CONV_CH), F32)],
        compiler_params=_params(1),
        name="inproj_sgu",
    )(h, gpre, w_main, w_dt, lng, sgu_w, sgu_b, conv_w, conv_b, dtb)


def _softplus(x):
    return jnp.maximum(x, 0.0) + jnp.log1p(jnp.exp(-jnp.abs(x)))


def _cumsum_rows(tril, a):
    hi = a.astype(BF16)
    r1 = a - hi.astype(F32)
    mid = r1.astype(BF16)
    lo = (r1 - mid.astype(F32)).astype(BF16)
    return _dot(tril, hi) + _dot(tril, mid) + _dot(tril, lo)


def _ssd_body(z_ref, xs_ref, bc_ref, dt_ref, alog_ref, d_ref, ng_ref, o_ref, hs_ref):
    @pl.when(pl.program_id(1) == 0)
    def _():
        hs_ref[...] = jnp.zeros_like(hs_ref)

    row = lax.broadcasted_iota(jnp.int32, (CHUNK, CHUNK), 0)
    col = lax.broadcasted_iota(jnp.int32, (CHUNK, CHUNK), 1)
    causal = row >= col
    tril = jnp.where(causal, 1.0, 0.0).astype(BF16)
    low_half = col < HEAD_DIM
    lane_head = lax.broadcasted_iota(jnp.int32, (1, GROUP_W), 1) // HEAD_DIM
    a_neg = -jnp.exp(alog_ref[...])
    c_off = SSM_GROUPS * SSM_STATE

    for c in range(SSD_TILE // CHUNK):
        r0 = c * CHUNK
        xs = xs_ref[0, r0:r0 + CHUNK, :]
        dt = dt_ref[0, r0:r0 + CHUNK, :]
        acs = _cumsum_rows(tril, dt * a_neg)
        acs_last = acs[CHUNK - 1:CHUNK, :]
        eacs = jnp.exp(acs)
        x = xs * dt
        xb = x.astype(BF16)
        xd = (x * jnp.exp(acs_last - acs)).astype(BF16)
        for g in range(SSM_GROUPS):
            gl = slice(g * GROUP_W, (g + 1) * GROUP_W)
            bg = bc_ref[0, r0:r0 + CHUNK, g * SSM_STATE:(g + 1) * SSM_STATE]
            cg = bc_ref[0, r0:r0 + CHUNK, c_off + g * SSM_STATE:c_off + (g + 1) * SSM_STATE]
            cbm = _dot_nt(cg, bg)
            xg = xb[:, gl]
            weights = []
            for e in range(SSM_HEADS // SSM_GROUPS):
                head = g * (SSM_HEADS // SSM_GROUPS) + e
                pair = acs[:, (head // 2) * 2 * HEAD_DIM:(head // 2 + 1) * 2 * HEAD_DIM]
                swapped = pltpu.roll(pair, HEAD_DIM, 1)
                acs_col = jnp.where(low_half, pair, swapped) if head % 2 == 0 else jnp.where(low_half, swapped, pair)
                seg = acs_col - acs_col.T
                decay = jnp.exp(jnp.where(causal, seg, -jnp.inf))
                weights.append((cbm * decay).astype(BF16))
            x_blocks = [jnp.where(lane_head == e, xg, jnp.zeros_like(xg)) for e in range(SSM_HEADS // SSM_GROUPS)]
            yg = _dot(jnp.concatenate(weights, axis=1), jnp.concatenate(x_blocks, axis=0))
            h_prev = hs_ref[g]
            y_off = _dot(cg, h_prev.astype(BF16)) * eacs[:, gl]
            hs_ref[g] = h_prev * eacs[CHUNK - 1:CHUNK, gl] + _dot(bg.astype(F32).T.astype(BF16), xd[:, gl])
            y = yg + y_off + d_ref[:, gl] * xs[:, gl]
            y = y * _silu(z_ref[0, r0:r0 + CHUNK, gl])
            o_ref[0, r0:r0 + CHUNK, gl] = _rms(y, ng_ref[:, gl])


def _ssd(z, xs, bc, dt, alog, dskip, norm_g, layer):
    b, s, _ = z.shape
    tile = lambda w: pl.BlockSpec((1, SSD_TILE, w), lambda i, j: (i, j, 0))
    vec = lambda w: _const_spec((None, 1, w), (layer, 0, 0))
    return pl.pallas_call(
        _ssd_body,
        out_shape=jax.ShapeDtypeStruct((b, s, SSM_INNER), F32),
        grid=(b, s // SSD_TILE),
        in_specs=[tile(SSM_INNER), tile(SSM_INNER), tile(SSM_CONV_CH - SSM_INNER), tile(SSM_INNER),
                  vec(SSM_INNER), vec(SSM_INNER), vec(SSM_INNER)],
        out_specs=tile(SSM_INNER),
        scratch_shapes=[pltpu.VMEM((SSM_GROUPS, SSM_STATE, GROUP_W), F32)],
        compiler_params=_params(2),
        name="ssd",
    )(z, xs, bc, dt, alog, dskip, norm_g)


def _attn_body(q_ref, kp_ref, kc_ref, vp_ref, vc_ref, bias_ref, o_ref, lse_ref, *, tq, ncls):
    n = pl.program_id(2)
    nblk = tq // CHUNK
    w = ATTN_WIDTH
    lane_head = lax.broadcasted_iota(jnp.int32, (1, w), 1) // HEAD_DIM
    first_table = jnp.where(n == 0, 1, 0)
    for c in range(ncls):
        cl = slice(c * w, (c + 1) * w)
        for j in range(nblk):
            rows = slice(j * CHUNK, (j + 1) * CHUNK)
            q = q_ref[0, rows, cl]
            lhs = jnp.concatenate([jnp.where(lane_head == h, q, jnp.zeros_like(q)) for h in range(ATTN_HEADS)], axis=0)
            if j == 0:
                kk = jnp.concatenate([kp_ref[0, :, cl], kc_ref[0, 0:CHUNK, cl]], axis=0)
                vv = jnp.concatenate([vp_ref[0, :, cl], vc_ref[0, 0:CHUNK, cl]], axis=0)
            else:
                kk = kc_ref[0, (j - 1) * CHUNK:(j + 1) * CHUNK, cl]
                vv = vc_ref[0, (j - 1) * CHUNK:(j + 1) * CHUNK, cl]
            logits = _dot_nt(lhs, kk) + bias_ref[first_table if j == 0 else 0]
            m = jnp.max(logits, axis=-1, keepdims=True)
            p = jnp.exp2(logits - m)
            s = jnp.sum(p, axis=-1, keepdims=True)
            pv = _dot(p.astype(BF16), vv) * (1.0 / s)
            lse = m + jnp.log2(s)
            o_acc = pv[0:CHUNK]
            lse_acc = jnp.broadcast_to(lse[0:CHUNK], (CHUNK, w))
            for h in range(1, ATTN_HEADS):
                hr = slice(h * CHUNK, (h + 1) * CHUNK)
                o_acc = jnp.where(lane_head == h, pv[hr], o_acc)
                lse_acc = jnp.where(lane_head == h, lse[hr], lse_acc)
            o_ref[0, rows, cl] = o_acc
            lse_ref[0, rows, cl] = lse_acc


def _attn_branch(qr, kr, vr, biases, branch):
    dil = DILATED_PATTERNS[branch][1]
    w = ATTN_WIDTH
    b, cls_len, _ = qr.shape
    tq = min(ATTN_TILE, cls_len)
    ncls = min(dil, ATTN_TILE // tq)
    cur = pl.BlockSpec((1, tq, ncls * w), lambda i, r, n: (i, n, r))
    prev = pl.BlockSpec((1, CHUNK, ncls * w), lambda i, r, n: (i, jnp.maximum(n * (tq // CHUNK) - 1, 0), r))
    o, lse = pl.pallas_call(
        functools.partial(_attn_body, tq=tq, ncls=ncls),
        out_shape=[jax.ShapeDtypeStruct((b, cls_len, dil * w), F32)] * 2,
        grid=(b, dil // ncls, cls_len // tq),
        in_specs=[cur, prev, cur, prev, cur, _const_spec((None, 2, ATTN_HEADS * CHUNK, 2 * CHUNK), (branch, 0, 0, 0))],
        out_specs=[cur, cur],
        compiler_params=_params(3),
        name=f"dilated_attn_{dil}",
    )(qr, kr, kr, vr, vr, biases)
    return o, lse


def _t5_bucket(dist):
    max_exact = REL_BUCKETS // 2
    d = jnp.maximum(dist, 1).astype(F32)
    large = max_exact + (jnp.log(d / max_exact) / math.log(REL_MAX_DIST / max_exact)
                         * (REL_BUCKETS - max_exact)).astype(jnp.int32)
    large = jnp.minimum(large, REL_BUCKETS - 1)
    return jnp.where(dist < max_exact, dist, large)


def _bias_body(rel_ref, span_ref, bucket_ref, o_ref):
    bucket = bucket_ref[0]
    row = lax.broadcasted_iota(jnp.int32, (CHUNK, 2 * CHUNK), 0)
    col = lax.broadcasted_iota(jnp.int32, (CHUNK, 2 * CHUNK), 1)
    dist = row + CHUNK - col
    band = (dist >= 0) & (dist <= span_ref[pl.program_id(0)])
    band_cur = band & (col >= CHUNK)
    for h in range(ATTN_HEADS):
        bias = jnp.zeros((CHUNK, 2 * CHUNK), F32)
        for b in range(REL_BUCKETS):
            bias = jnp.where(bucket == b, rel_ref[b, h], bias)
        bias = bias * LOG2E
        o_ref[0, 0, h * CHUNK:(h + 1) * CHUNK, :] = jnp.where(band, bias, -jnp.inf)
        o_ref[0, 1, h * CHUNK:(h + 1) * CHUNK, :] = jnp.where(band_cur, bias, -jnp.inf)


def _bias_tables(rel_bias):
    assert all(window // dil <= CHUNK for window, dil in DILATED_PATTERNS)
    dist = jnp.arange(CHUNK)[:, None] + CHUNK - jnp.arange(2 * CHUNK)[None, :]
    buckets = jnp.stack([_t5_bucket(jnp.maximum(dist, 0) * dil) for _, dil in DILATED_PATTERNS]).astype(jnp.int32)
    spans = jnp.array([window // dil for window, dil in DILATED_PATTERNS], jnp.int32)
    nb = len(DILATED_PATTERNS)
    return pl.pallas_call(
        _bias_body,
        out_shape=jax.ShapeDtypeStruct((nb, 2, ATTN_HEADS * CHUNK, 2 * CHUNK), F32),
        grid=(nb,),
        in_specs=[pl.BlockSpec(memory_space=pltpu.SMEM), pl.BlockSpec(memory_space=pltpu.SMEM),
                  pl.BlockSpec((1, CHUNK, 2 * CHUNK), lambda i: (i, 0, 0))],
        out_specs=pl.BlockSpec((1, 2, ATTN_HEADS * CHUNK, 2 * CHUNK), lambda i: (i, 0, 0, 0)),
        compiler_params=_params(1),
        name="rel_bias_tables",
    )(rel_bias, spans, buckets)


def _mix_xattn_body(h_ref, a_ref, b_ref, *rest):
    nb = len(DILATED_PATTERNS)
    branch_refs = rest[:2 * nb]
    w_ref, gpost_ref, xpre_ref, wq_ref, k_ref, v_ref, wo_ref, xpost_ref, out_ref, nat_ref = rest[2 * nb:]
    n_slab = ATTN_WIDTH // LANES
    off_b = SGU_WIDTH
    off_c = SGU_WIDTH + SSM_INNER
    hd = h_ref.shape[-1] // XATTN_HEADS

    def half(i):
        rows = slice(i * MIX_HALF, (i + 1) * MIX_HALF)

        def token_order(idx):
            ref, dil = branch_refs[idx], DILATED_PATTERNS[idx % nb][1]
            blk = slice(i * MIX_HALF // dil, (i + 1) * MIX_HALF // dil)
            if dil == 1:
                return ref[blk, :]
            slot = (i * 2 * nb + idx) * n_slab
            for r in range(dil):
                for sl in range(n_slab):
                    lo = r * ATTN_WIDTH + sl * LANES
                    nat_ref[slot + sl, pl.ds(r, MIX_HALF // dil, stride=dil), :] = ref[blk, lo:lo + LANES]
            return jnp.concatenate([nat_ref[slot + sl] for sl in range(n_slab)], axis=-1)

        y = (_dot(a_ref[rows, :].astype(BF16), w_ref[0:off_b, :])
             + _dot(b_ref[rows, :].astype(BF16), w_ref[off_b:off_c, :]))
        o1, o2, o3 = (token_order(j) for j in range(nb))
        l1, l2, l3 = (token_order(nb + j) for j in range(nb))
        m = jnp.maximum(jnp.maximum(l1, l2), l3)
        e1, e2, e3 = jnp.exp2(l1 - m), jnp.exp2(l2 - m), jnp.exp2(l3 - m)
        c = (e1 * o1 + e2 * o2 + e3 * o3) * (1.0 / (e1 + e2 + e3))
        yield
        y = y + _dot(c.astype(BF16), w_ref[off_c:off_c + ATTN_WIDTH, :])
        h = h_ref[rows, :] + _rms(y, gpost_ref[...])
        hn = _rms(h, xpre_ref[...]).astype(BF16)
        yield
        q = (_dot(hn, wq_ref[...]) * (hd ** -0.5)).astype(BF16)
        yield
        heads = []
        for j in range(XATTN_HEADS):
            hs = slice(j * hd, (j + 1) * hd)
            logits = _dot_nt(q[:, hs], k_ref[0, :, hs])
            m = jnp.max(logits, axis=-1, keepdims=True)
            p = jnp.exp(logits - m)
            s = jnp.sum(p, axis=-1, keepdims=True)
            heads.append((_dot(p.astype(BF16), v_ref[0, :, hs]) * (1.0 / s)).astype(BF16))
        yield
        y = _dot(jnp.concatenate(heads, axis=-1), wo_ref[...])
        out_ref[rows, :] = h + _rms(y, xpost_ref[...])
        yield

    n_half = MIX_ROW_TILE // MIX_HALF
    gens = [half(i) for i in range(n_half)]
    n_stages = 5
    for t in range(n_stages + n_half - 1):
        for i, g in enumerate(gens):
            if 0 <= t - i < n_stages:
                next(g)


def _mix_xattn(h, a, bo, outs, lses, w_out, npost, npre, wq, k, v, wo, layer, tiles_per_seq):
    rows, d = h.shape
    m = k.shape[1]
    row_spec = lambda w, dil=1: pl.BlockSpec((MIX_ROW_TILE // dil, w * dil), lambda i: (i, 0))
    attn = [row_spec(ATTN_WIDTH, dil) for _, dil in DILATED_PATTERNS]
    kv = pl.BlockSpec((1, m, d), lambda i: (i // tiles_per_seq, 0, 0))
    flat = lambda t: t.reshape(-1, t.shape[-1])
    n_slabs = (MIX_ROW_TILE // MIX_HALF) * 2 * len(DILATED_PATTERNS) * ATTN_WIDTH // LANES
    return pl.pallas_call(
        _mix_xattn_body,
        out_shape=jax.ShapeDtypeStruct((rows, d), F32),
        grid=(rows // MIX_ROW_TILE,),
        in_specs=[row_spec(d), row_spec(SGU_WIDTH), row_spec(SSM_INNER), *attn, *attn,
                  _const_spec((None, d, d), (layer, 0, 0)),
                  _const_spec((None, None, 1, d), (layer, 1, 0, 0)),
                  _const_spec((None, None, 1, d), (layer, 2, 0, 0)),
                  _const_spec((None, d, d), (layer, 0, 0)),
                  kv, kv,
                  _const_spec((None, d, d), (layer, 0, 0)),
                  _const_spec((None, None, 1, d), (layer, 2, 0, 0))],
        out_specs=row_spec(d),
        scratch_shapes=[pltpu.VMEM((n_slabs, MIX_HALF, LANES), F32)],
        compiler_params=_params(1),
        name="mix_out_xattn",
    )(h, a, bo, *map(flat, outs), *map(flat, lses), w_out, npost, npre, wq, k, v, wo, npost)


def _memkv_body(mem_ref, g_ref, w_ref, k_ref, v_ref):
    d = mem_ref.shape[-1]
    mn = _rms(mem_ref[0], g_ref[...]).astype(BF16)
    k_ref[0] = _dot(mn, w_ref[:, 0:d]).astype(BF16)
    v_ref[0] = _dot(mn, w_ref[:, d:2 * d]).astype(BF16)


def _memkv(mem, g, wkv, layer):
    b, m, d = mem.shape
    blk = pl.BlockSpec((1, m, d), lambda i: (i, 0, 0))
    return pl.pallas_call(
        _memkv_body,
        out_shape=[jax.ShapeDtypeStruct((b, m, d), BF16)] * 2,
        grid=(b,),
        in_specs=[blk, _const_spec((None, 1, d), (layer, 0, 0)), _const_spec((None, d, 2 * d), (layer, 0, 0))],
        out_specs=[blk, blk],
        compiler_params=_params(1),
        name="mem_kv",
    )(mem, g, wkv)


def kernel(x, mem, norm_pre, norm_post, ffn_wi, ffn_wo, mix_w_in, mix_w_out, sgu_ln_g, sgu_w, sgu_b, ssm_conv_w, ssm_conv_b, ssm_dt_bias, ssm_a_log, ssm_d, ssm_norm_g, rel_bias, mem_norm_g, xattn_wq, xattn_wkv, xattn_wo):
    b, s, d = x.shape
    depth = norm_pre.shape[0]
    assert s % ATTN_TILE == 0 and s % ROW_TILE == 0 and s % SSD_TILE == 0
    assert all(s % (dil * CHUNK) == 0 for _, dil in DILATED_PATTERNS)

    npre = norm_pre[:, :, None, :]
    npost = norm_post[:, :, None, :]
    wi = ffn_wi.astype(BF16)
    wo = ffn_wo.astype(BF16)
    w_main = jnp.concatenate([mix_w_in[..., :OFF_DT], mix_w_in[..., OFF_QKV:]], axis=-1).astype(BF16)
    w_dt = jnp.repeat(mix_w_in[..., OFF_DT:OFF_QKV], HEAD_DIM, axis=-1).astype(BF16)
    w_out = mix_w_out.astype(BF16)
    lng = sgu_ln_g[:, None, :]
    sgu_b_e = jnp.repeat(jnp.swapaxes(sgu_b, 1, 2), HEAD_DIM, axis=-1)
    per_head = lambda p: jnp.repeat(p, HEAD_DIM, axis=-1)[:, None, :]
    dtb, alog, dskip = per_head(ssm_dt_bias), per_head(ssm_a_log), per_head(ssm_d)
    conv_b = ssm_conv_b[:, None, :]
    norm_g = ssm_norm_g[:, None, :]
    mem_g = mem_norm_g[:, None, :]
    wq, wkv, wxo = xattn_wq.astype(BF16), xattn_wkv.astype(BF16), xattn_wo.astype(BF16)
    biases = _bias_tables(rel_bias)

    h = x.reshape(b * s, d)
    for l in range(depth):
        h = _ffn(h, npre, wi, wo, npost, l, 0)
        a, z, xs, bc, dt, *qkv = _inproj(h, npre, w_main, w_dt, lng, sgu_w, sgu_b_e, ssm_conv_w, conv_b, dtb, l,
                                         s // ROW_TILE)
        seq = lambda t: t.reshape(b, -1, t.shape[-1])
        bo = _ssd(seq(z), seq(xs), seq(bc), seq(dt), alog, dskip, norm_g, l)
        outs, lses = [], []
        for branch in range(len(DILATED_PATTERNS)):
            o, lse = _attn_branch(*map(seq, qkv[3 * branch:3 * branch + 3]), biases, branch)
            outs.append(o)
            lses.append(lse)
        mk, mv = _memkv(mem, mem_g, wkv, l)
        h = _mix_xattn(h, a, bo.reshape(b * s, SSM_INNER), outs, lses, w_out, npost, npre, wq, mk, mv, wxo, l,
                       s // MIX_ROW_TILE)
        h = _ffn(h, npre, wi, wo, npost, l, 1)
    return h.reshape(b, s, d)
```

```python
import functools
import math

import jax
import jax.numpy as jnp
from jax import lax
from jax.experimental import pallas as pl
from jax.experimental.pallas import tpu as pltpu

F32 = jnp.float32
BF16 = jnp.bfloat16
EPS = 1e-6
LOG2E = math.log2(math.e)

LANES = 128
MXU_N = 256
HEAD_DIM = 64
CHUNK = 128
SGU_WIDTH = 256
SGU_GROUPS = SGU_WIDTH // HEAD_DIM
SSM_INNER = 512
SSM_HEADS = SSM_INNER // HEAD_DIM
SSM_GROUPS = 2
SSM_STATE = 128
SSM_CONV = 4
SSM_CONV_CH = SSM_INNER + 2 * SSM_GROUPS * SSM_STATE
GROUP_W = SSM_INNER // SSM_GROUPS
ATTN_WIDTH = 256
ATTN_HEADS = ATTN_WIDTH // HEAD_DIM
DILATED_PATTERNS = ((128, 1), (512, 4), (2048, 16))
REL_BUCKETS = 32
REL_MAX_DIST = 2048
XATTN_HEADS = 4

OFF_Z = 2 * SGU_WIDTH
OFF_XBC = OFF_Z + SSM_INNER
OFF_DT = OFF_XBC + SSM_CONV_CH
OFF_QKV = OFF_DT + SSM_HEADS
W_MAIN = OFF_DT + 3 * ATTN_WIDTH

ROW_TILE = 512
FFN_ROW_TILE = 1024
FF_TILE = 256
SSD_TILE = 1024
ATTN_TILE = 2048
CONV_PAD = 8
VMEM_LIMIT = 56 * 1024 * 1024


def _params(n_axes):
    return pltpu.CompilerParams(dimension_semantics=("arbitrary",) * n_axes,
                                vmem_limit_bytes=VMEM_LIMIT)


def _rms(x, g):
    return x * lax.rsqrt(jnp.mean(x * x, axis=-1, keepdims=True) + EPS) * g


def _silu(x):
    return x * jax.nn.sigmoid(x)


def _dot(a, b):
    return jnp.dot(a, b, preferred_element_type=F32)


def _dot_nt(a, b):
    return lax.dot_general(a, b, (((1,), (1,)), ((), ())), preferred_element_type=F32)


def _const_spec(shape, index):
    return pl.BlockSpec(shape, lambda *_: index, pipeline_mode=pl.Buffered(1))


def _ffn_body(h_ref, gpre_ref, wi_ref, wo_ref, gpost_ref, o_ref, acc_ref, *, d_ff):
    for r0 in range(0, FFN_ROW_TILE, ROW_TILE):
        rows = slice(r0, r0 + ROW_TILE)
        h = h_ref[rows, :]
        hn = _rms(h, gpre_ref[...]).astype(BF16)
        for c in range(d_ff // FF_TILE):
            lo = c * FF_TILE
            g = _dot(hn, wi_ref[:, lo:lo + FF_TILE])
            u = _dot(hn, wi_ref[:, d_ff + lo:d_ff + lo + FF_TILE])
            y = _dot((_silu(g) * u).astype(BF16), wo_ref[lo:lo + FF_TILE, :])
            if c == 0:
                acc_ref[rows, :] = y
            else:
                acc_ref[rows, :] += y
        o_ref[rows, :] = h + 0.5 * _rms(acc_ref[rows, :], gpost_ref[...])


def _ffn(h, gpre, wi, wo, gpost, layer, which):
    rows, d = h.shape
    d_ff = wo.shape[2]
    return pl.pallas_call(
        functools.partial(_ffn_body, d_ff=d_ff),
        out_shape=jax.ShapeDtypeStruct((rows, d), F32),
        grid=(rows // FFN_ROW_TILE,),
        in_specs=[
            pl.BlockSpec((FFN_ROW_TILE, d), lambda i: (i, 0)),
            _const_spec((None, None, 1, d), (layer, 3 * which, 0, 0)),
            _const_spec((None, None, d, 2 * d_ff), (layer, which, 0, 0)),
            _const_spec((None, None, d_ff, d), (layer, which, 0, 0)),
            _const_spec((None, None, 1, d), (layer, 3 * which, 0, 0)),
        ],
        out_specs=pl.BlockSpec((FFN_ROW_TILE, d), lambda i: (i, 0)),
        scratch_shapes=[pltpu.VMEM((FFN_ROW_TILE, d), F32)],
        compiler_params=_params(1),
        name="ffn",
    )(h, gpre, wi, wo, gpost)


def _inproj_body(h_ref, gpre_ref, w_ref, wdt_ref, lng_ref, sw_ref, sb_ref, cw_ref, cb_ref, dtb_ref,
                 a_ref, zxd_ref, bc_ref, *rest, tiles_per_seq):
    kvq_refs, (slab_ref, ext_ref) = rest[:-2], rest[-2:]

    first = pl.program_id(0) % tiles_per_seq == 0

    @pl.when(first)
    def _():
        ext_ref[0:CONV_PAD, :] = jnp.zeros((CONV_PAD, SSM_CONV_CH), F32)

    @pl.when(jnp.logical_not(first))
    def _():
        ext_ref[0:CONV_PAD, :] = ext_ref[ROW_TILE:ROW_TILE + CONV_PAD, :]

    hn = _rms(h_ref[...], gpre_ref[...]).astype(BF16)

    def proj(lo, hi):
        return _dot(hn, w_ref[:, lo:hi])

    def stage_xbc(g):
        lo = g * MXU_N
        ext_ref[CONV_PAD:CONV_PAD + ROW_TILE, lo:lo + MXU_N] = proj(OFF_XBC + lo, OFF_XBC + lo + MXU_N)

    def conv_chunk(c, lo, hi, out_ref, out_lo):
        r0 = c * CHUNK
        for l0 in range(lo, hi, LANES):
            ls = slice(l0, l0 + LANES)
            win = ext_ref[r0:r0 + CONV_PAD + CHUNK, ls]
            acc = cb_ref[:, ls] + win[CONV_PAD:] * cw_ref[SSM_CONV - 1:SSM_CONV, ls]
            for back in range(1, SSM_CONV):
                tap = SSM_CONV - 1 - back
                acc = acc + pltpu.roll(win, back, 0)[CONV_PAD:] * cw_ref[tap:tap + 1, ls]
            o0 = out_lo + l0 - lo
            out_ref[r0:r0 + CHUNK, o0:o0 + LANES] = _silu(acc).astype(out_ref.dtype)

    conv_x = lambda c: conv_chunk(c, 0, SSM_INNER, zxd_ref, SSM_INNER)
    conv_bc = lambda c: conv_chunk(c, SSM_INNER, SSM_CONV_CH, bc_ref, 0)

    def dt_half(g):
        ls = slice(g * MXU_N, (g + 1) * MXU_N)
        zxd_ref[:, 2 * SSM_INNER + g * MXU_N:2 * SSM_INNER + (g + 1) * MXU_N] = (
            _softplus(_dot(hn, wdt_ref[:, ls]) + dtb_ref[:, ls]))

    uv = proj(0, OFF_Z)
    stage_xbc(0)
    stage_xbc(1)

    uv = jax.nn.gelu(uv)
    u = uv[:, :SGU_WIDTH]
    v = uv[:, SGU_WIDTH:]
    mu = jnp.mean(v, axis=-1, keepdims=True)
    var = jnp.mean(jnp.square(v - mu), axis=-1, keepdims=True)
    vn = ((v - mu) * lax.rsqrt(var + EPS) * lng_ref[...]).astype(BF16)
    causal = (lax.broadcasted_iota(jnp.int32, (CHUNK, CHUNK), 0)
              >= lax.broadcasted_iota(jnp.int32, (CHUNK, CHUNK), 1))
    lane_group = lax.broadcasted_iota(jnp.int32, (1, SGU_WIDTH), 1) // HEAD_DIM
    w_cat = jnp.concatenate([jnp.where(causal, sw_ref[g], 0.0).astype(BF16) for g in range(SGU_GROUPS)], axis=1)
    for c in range(ROW_TILE // CHUNK):
        r0 = c * CHUNK
        vc = vn[r0:r0 + CHUNK]
        v_blocks = [jnp.where(lane_group == g, vc, jnp.zeros_like(vc)) for g in range(SGU_GROUPS)]
        s = sb_ref[...] + _dot(w_cat, jnp.concatenate(v_blocks, axis=0))
        a_ref[r0:r0 + CHUNK, :] = u[r0:r0 + CHUNK] * s

    stage_xbc(2)
    conv_x(0)
    stage_xbc(3)
    conv_x(1)

    n_slab = ATTN_WIDTH // LANES
    for t, vector_step in enumerate((lambda: conv_x(2), lambda: conv_x(3), lambda: conv_bc(0))):
        val = proj(OFF_DT + t * ATTN_WIDTH, OFF_DT + (t + 1) * ATTN_WIDTH)
        if t == 0:
            val = val * (HEAD_DIM ** -0.5 * LOG2E)
        for sl in range(n_slab):
            slab_ref[t * n_slab + sl] = val[:, sl * LANES:(sl + 1) * LANES]
        plane = (t + 2) % 3
        for bi, (_, dil) in enumerate(DILATED_PATTERNS):
            out = kvq_refs[bi]
            if dil == 1:
                out[plane] = val.astype(BF16)
                continue
            for r in range(dil):
                for sl in range(n_slab):
                    lo = r * ATTN_WIDTH + sl * LANES
                    out[plane, :, lo:lo + LANES] = (
                        slab_ref[t * n_slab + sl, pl.ds(r, ROW_TILE // dil, stride=dil), :].astype(BF16))
        vector_step()

    dt_half(0)
    conv_bc(1)
    dt_half(1)
    conv_bc(2)
    zxd_ref[:, 0:MXU_N] = proj(OFF_Z, OFF_Z + MXU_N)
    conv_bc(3)
    zxd_ref[:, MXU_N:SSM_INNER] = proj(OFF_Z + MXU_N, OFF_XBC)


def _inproj(h, gpre, w_main, w_dt, lng, sgu_w, sgu_b, conv_w, conv_b, dtb, layer, tiles_per_seq):
    rows, d = h.shape
    row_spec = lambda w: pl.BlockSpec((ROW_TILE, w), lambda i: (i, 0))
    vec = lambda w: _const_spec((None, 1, w), (layer, 0, 0))
    outs = ((SGU_WIDTH, F32), (3 * SSM_INNER, F32), (SSM_CONV_CH - SSM_INNER, BF16))
    out_shape = [jax.ShapeDtypeStruct((rows, w), t) for w, t in outs]
    out_specs = [row_spec(w) for w, _ in outs]
    for _, dil in DILATED_PATTERNS:
        out_shape.append(jax.ShapeDtypeStruct((3, rows // dil, ATTN_WIDTH * dil), BF16))
        out_specs.append(pl.BlockSpec((3, ROW_TILE // dil, ATTN_WIDTH * dil), lambda i: (0, i, 0)))
    return pl.pallas_call(
        functools.partial(_inproj_body, tiles_per_seq=tiles_per_seq),
        out_shape=out_shape,
        grid=(rows // ROW_TILE,),
        in_specs=[
            row_spec(d),
            _const_spec((None, None, 1, d), (layer, 1, 0, 0)),
            _const_spec((None, d, W_MAIN), (layer, 0, 0)),
            _const_spec((None, d, SSM_INNER), (layer, 0, 0)),
            vec(SGU_WIDTH),
            _const_spec((None, SGU_GROUPS, CHUNK, CHUNK), (layer, 0, 0, 0)),
            _const_spec((None, CHUNK, SGU_WIDTH), (layer, 0, 0)),
            _const_spec((None, SSM_CONV, SSM_CONV_CH), (layer, 0, 0)),
            vec(SSM_CONV_CH), vec(SSM_INNER),
        ],
        out_specs=out_specs,
        scratch_shapes=[pltpu.VMEM((3 * ATTN_WIDTH // LANES, ROW_TILE, LANES), F32),
                        pltpu.VMEM((CONV_PAD + ROW_TILE, SSM_CONV_CH), F32)],
        compiler_params=_params(1),
        name="inproj_sgu",
    )(h, gpre, w_main, w_dt, lng, sgu_w, sgu_b, conv_w, conv_b, dtb)


def _softplus(x):
    return jnp.maximum(x, 0.0) + jnp.log1p(jnp.exp(-jnp.abs(x)))


def _cumsum_rows(tril, a):
    hi = a.astype(BF16)
    r1 = a - hi.astype(F32)
    mid = r1.astype(BF16)
    lo = (r1 - mid.astype(F32)).astype(BF16)
    return _dot(tril, hi) + _dot(tril, mid) + _dot(tril, lo)


def _ssd_body(zxd_ref, bc_ref, alog_ref, d_ref, ng_ref, o_ref, hs_ref):
    @pl.when(pl.program_id(1) == 0)
    def _():
        hs_ref[...] = jnp.zeros_like(hs_ref)

    row = lax.broadcasted_iota(jnp.int32, (CHUNK, CHUNK), 0)
    col = lax.broadcasted_iota(jnp.int32, (CHUNK, CHUNK), 1)
    causal = row >= col
    tril = jnp.where(causal, 1.0, 0.0).astype(BF16)
    low_half = col < HEAD_DIM
    lane_head = lax.broadcasted_iota(jnp.int32, (1, GROUP_W), 1) // HEAD_DIM
    a_neg = -jnp.exp(alog_ref[...])
    c_off = SSM_GROUPS * SSM_STATE

    for c in range(SSD_TILE // CHUNK):
        r0 = c * CHUNK
        xs = zxd_ref[0, r0:r0 + CHUNK, SSM_INNER:2 * SSM_INNER]
        dt = zxd_ref[0, r0:r0 + CHUNK, 2 * SSM_INNER:3 * SSM_INNER]
        acs = _cumsum_rows(tril, dt * a_neg)
        acs_last = acs[CHUNK - 1:CHUNK, :]
        eacs = jnp.exp(acs)
        x = xs * dt
        xb = x.astype(BF16)
        xd = (x * jnp.exp(acs_last - acs)).astype(BF16)
        for g in range(SSM_GROUPS):
            gl = slice(g * GROUP_W, (g + 1) * GROUP_W)
            bg = bc_ref[0, r0:r0 + CHUNK, g * SSM_STATE:(g + 1) * SSM_STATE]
            cg = bc_ref[0, r0:r0 + CHUNK, c_off + g * SSM_STATE:c_off + (g + 1) * SSM_STATE]
            cbm = _dot_nt(cg, bg)
            xg = xb[:, gl]
            weights = []
            for e in range(SSM_HEADS // SSM_GROUPS):
                head = g * (SSM_HEADS // SSM_GROUPS) + e
                pair = acs[:, (head // 2) * 2 * HEAD_DIM:(head // 2 + 1) * 2 * HEAD_DIM]
                swapped = pltpu.roll(pair, HEAD_DIM, 1)
                acs_col = jnp.where(low_half, pair, swapped) if head % 2 == 0 else jnp.where(low_half, swapped, pair)
                seg = acs_col - acs_col.T
                decay = jnp.exp(jnp.where(causal, seg, -jnp.inf))
                weights.append((cbm * decay).astype(BF16))
            x_blocks = [jnp.where(lane_head == e, xg, jnp.zeros_like(xg)) for e in range(SSM_HEADS // SSM_GROUPS)]
            yg = _dot(jnp.concatenate(weights, axis=1), jnp.concatenate(x_blocks, axis=0))
            h_prev = hs_ref[g]
            y_off = _dot(cg, h_prev.astype(BF16)) * eacs[:, gl]
            hs_ref[g] = h_prev * eacs[CHUNK - 1:CHUNK, gl] + _dot(bg.astype(F32).T.astype(BF16), xd[:, gl])
            y = yg + y_off + d_ref[:, gl] * xs[:, gl]
            y = y * _silu(zxd_ref[0, r0:r0 + CHUNK, gl])
            o_ref[0, r0:r0 + CHUNK, gl] = _rms(y, ng_ref[:, gl])


def _ssd(zxd, bc, alog, dskip, norm_g, layer):
    b, s, _ = zxd.shape
    tile = lambda w: pl.BlockSpec((1, SSD_TILE, w), lambda i, j: (i, j, 0))
    vec = lambda w: _const_spec((None, 1, w), (layer, 0, 0))
    return pl.pallas_call(
        _ssd_body,
        out_shape=jax.ShapeDtypeStruct((b, s, SSM_INNER), F32),
        grid=(b, s // SSD_TILE),
        in_specs=[tile(3 * SSM_INNER), tile(SSM_CONV_CH - SSM_INNER), vec(SSM_INNER), vec(SSM_INNER), vec(SSM_INNER)],
        out_specs=tile(SSM_INNER),
        scratch_shapes=[pltpu.VMEM((SSM_GROUPS, SSM_STATE, GROUP_W), F32)],
        compiler_params=_params(2),
        name="ssd",
    )(zxd, bc, alog, dskip, norm_g)


def _attn_body(q_ref, kvp_ref, kvc_ref, bias_ref, ol_ref, *, tq, ncls):
    n = pl.program_id(2)
    nblk = tq // CHUNK
    w = ATTN_WIDTH
    lane_head = lax.broadcasted_iota(jnp.int32, (1, w), 1) // HEAD_DIM
    first_table = jnp.where(n == 0, 1, 0)
    for c in range(ncls):
        cl = slice(c * w, (c + 1) * w)
        for j in range(nblk):
            rows = slice(j * CHUNK, (j + 1) * CHUNK)
            q = q_ref[0, rows, cl]
            lhs = jnp.concatenate([jnp.where(lane_head == h, q, jnp.zeros_like(q)) for h in range(ATTN_HEADS)], axis=0)
            if j == 0:
                kk, vv = (jnp.concatenate([kvp_ref[t, 0, :, cl], kvc_ref[t, 0, 0:CHUNK, cl]], axis=0) for t in (0, 1))
            else:
                kk, vv = (kvc_ref[t, 0, (j - 1) * CHUNK:(j + 1) * CHUNK, cl] for t in (0, 1))
            logits = _dot_nt(lhs, kk) + bias_ref[first_table if j == 0 else 0]
            m = jnp.max(logits, axis=-1, keepdims=True)
            p = jnp.exp2(logits - m)
            s = jnp.sum(p, axis=-1, keepdims=True)
            pv = _dot(p.astype(BF16), vv) * (1.0 / s)
            lse = m + jnp.log2(s)
            o_acc = pv[0:CHUNK]
            lse_acc = jnp.broadcast_to(lse[0:CHUNK], (CHUNK, w))
            for h in range(1, ATTN_HEADS):
                hr = slice(h * CHUNK, (h + 1) * CHUNK)
                o_acc = jnp.where(lane_head == h, pv[hr], o_acc)
                lse_acc = jnp.where(lane_head == h, lse[hr], lse_acc)
            ol_ref[0, 0, rows, cl] = o_acc
            ol_ref[1, 0, rows, cl] = lse_acc


def _attn_branch(kvq, biases, branch):
    dil = DILATED_PATTERNS[branch][1]
    w = ATTN_WIDTH
    _, b, cls_len, _ = kvq.shape
    tq = min(ATTN_TILE, cls_len)
    ncls = min(dil, ATTN_TILE // tq)
    prev_block = lambda n: jnp.maximum(n * (tq // CHUNK) - 1, 0)
    return pl.pallas_call(
        functools.partial(_attn_body, tq=tq, ncls=ncls),
        out_shape=jax.ShapeDtypeStruct((2, b, cls_len, dil * w), F32),
        grid=(b, dil // ncls, cls_len // tq),
        in_specs=[pl.BlockSpec((None, 1, tq, ncls * w), lambda i, r, n: (2, i, n, r)),
                  pl.BlockSpec((2, 1, CHUNK, ncls * w), lambda i, r, n: (0, i, prev_block(n), r)),
                  pl.BlockSpec((2, 1, tq, ncls * w), lambda i, r, n: (0, i, n, r)),
                  _const_spec((None, 2, ATTN_HEADS * CHUNK, 2 * CHUNK), (branch, 0, 0, 0))],
        out_specs=pl.BlockSpec((2, 1, tq, ncls * w), lambda i, r, n: (0, i, n, r)),
        compiler_params=_params(3),
        name=f"dilated_attn_{dil}",
    )(kvq, kvq, kvq, biases)


def _t5_bucket(dist):
    max_exact = REL_BUCKETS // 2
    d = jnp.maximum(dist, 1).astype(F32)
    large = max_exact + (jnp.log(d / max_exact) / math.log(REL_MAX_DIST / max_exact)
                         * (REL_BUCKETS - max_exact)).astype(jnp.int32)
    large = jnp.minimum(large, REL_BUCKETS - 1)
    return jnp.where(dist < max_exact, dist, large)


def _bias_body(rel_ref, span_ref, bucket_ref, o_ref):
    bucket = bucket_ref[0]
    row = lax.broadcasted_iota(jnp.int32, (CHUNK, 2 * CHUNK), 0)
    col = lax.broadcasted_iota(jnp.int32, (CHUNK, 2 * CHUNK), 1)
    dist = row + CHUNK - col
    band = (dist >= 0) & (dist <= span_ref[pl.program_id(0)])
    band_cur = band & (col >= CHUNK)
    for h in range(ATTN_HEADS):
        bias = jnp.zeros((CHUNK, 2 * CHUNK), F32)
        for b in range(REL_BUCKETS):
            bias = jnp.where(bucket == b, rel_ref[b, h], bias)
        bias = bias * LOG2E
        o_ref[0, 0, h * CHUNK:(h + 1) * CHUNK, :] = jnp.where(band, bias, -jnp.inf)
        o_ref[0, 1, h * CHUNK:(h + 1) * CHUNK, :] = jnp.where(band_cur, bias, -jnp.inf)


def _bias_tables(rel_bias):
    assert all(window // dil <= CHUNK for window, dil in DILATED_PATTERNS)
    dist = jnp.arange(CHUNK)[:, None] + CHUNK - jnp.arange(2 * CHUNK)[None, :]
    buckets = jnp.stack([_t5_bucket(jnp.maximum(dist, 0) * dil) for _, dil in DILATED_PATTERNS]).astype(jnp.int32)
    spans = jnp.array([window // dil for window, dil in DILATED_PATTERNS], jnp.int32)
    nb = len(DILATED_PATTERNS)
    return pl.pallas_call(
        _bias_body,
        out_shape=jax.ShapeDtypeStruct((nb, 2, ATTN_HEADS * CHUNK, 2 * CHUNK), F32),
        grid=(nb,),
        in_specs=[pl.BlockSpec(memory_space=pltpu.SMEM), pl.BlockSpec(memory_space=pltpu.SMEM),
                  pl.BlockSpec((1, CHUNK, 2 * CHUNK), lambda i: (i, 0, 0))],
        out_specs=pl.BlockSpec((1, 2, ATTN_HEADS * CHUNK, 2 * CHUNK), lambda i: (i, 0, 0, 0)),
        compiler_params=_params(1),
        name="rel_bias_tables",
    )(rel_bias, spans, buckets)


def _mix_xattn_body(h_ref, a_ref, b_ref, *rest):
    nb = len(DILATED_PATTERNS)
    branch_refs = rest[:nb]
    w_ref, gpost_ref, xpre_ref, wq_ref, kv_ref, wo_ref, xpost_ref, out_ref, nat_ref = rest[nb:]
    n_slab = ATTN_WIDTH // LANES
    off_b = SGU_WIDTH
    off_c = SGU_WIDTH + SSM_INNER
    d = h_ref.shape[-1]
    hd = d // XATTN_HEADS

    def token_order(bi, plane):
        ref, dil = branch_refs[bi], DILATED_PATTERNS[bi][1]
        if dil == 1:
            return ref[plane]
        slot = (plane * nb + bi) * n_slab
        for r in range(dil):
            for sl in range(n_slab):
                lo = r * ATTN_WIDTH + sl * LANES
                nat_ref[slot + sl, pl.ds(r, ROW_TILE // dil, stride=dil), :] = ref[plane, :, lo:lo + LANES]
        return jnp.concatenate([nat_ref[slot + sl] for sl in range(n_slab)], axis=-1)

    o1, o2, o3 = (token_order(bi, 0) for bi in range(nb))
    l1, l2, l3 = (token_order(bi, 1) for bi in range(nb))
    m = jnp.maximum(jnp.maximum(l1, l2), l3)
    e1, e2, e3 = jnp.exp2(l1 - m), jnp.exp2(l2 - m), jnp.exp2(l3 - m)
    c = (e1 * o1 + e2 * o2 + e3 * o3) * (1.0 / (e1 + e2 + e3))
    y = (_dot(a_ref[...].astype(BF16), w_ref[0:off_b, :])
         + _dot(b_ref[...].astype(BF16), w_ref[off_b:off_c, :])
         + _dot(c.astype(BF16), w_ref[off_c:off_c + ATTN_WIDTH, :]))
    h = h_ref[...] + _rms(y, gpost_ref[...])

    hn = _rms(h, xpre_ref[...]).astype(BF16)
    q = (_dot(hn, wq_ref[...]) * (hd ** -0.5)).astype(BF16)
    heads = []
    for i in range(XATTN_HEADS):
        hs = slice(i * hd, (i + 1) * hd)
        logits = _dot_nt(q[:, hs], kv_ref[0, :, hs])
        m = jnp.max(logits, axis=-1, keepdims=True)
        p = jnp.exp(logits - m)
        s = jnp.sum(p, axis=-1, keepdims=True)
        heads.append((_dot(p.astype(BF16), kv_ref[0, :, d + i * hd:d + (i + 1) * hd]) * (1.0 / s)).astype(BF16))
    y = _dot(jnp.concatenate(heads, axis=-1), wo_ref[...])
    out_ref[...] = h + _rms(y, xpost_ref[...])


def _mix_xattn(h, a, bo, branch_outs, w_out, npost, npre, wq, kv, wo, layer, tiles_per_seq):
    rows, d = h.shape
    m = kv.shape[1]
    row_spec = lambda w: pl.BlockSpec((ROW_TILE, w), lambda i: (i, 0))
    branch = [pl.BlockSpec((2, ROW_TILE // dil, ATTN_WIDTH * dil), lambda i: (0, i, 0)) for _, dil in DILATED_PATTERNS]
    return pl.pallas_call(
        _mix_xattn_body,
        out_shape=jax.ShapeDtypeStruct((rows, d), F32),
        grid=(rows // ROW_TILE,),
        in_specs=[row_spec(d), row_spec(SGU_WIDTH), row_spec(SSM_INNER), *branch,
                  _const_spec((None, d, d), (layer, 0, 0)),
                  _const_spec((None, None, 1, d), (layer, 1, 0, 0)),
                  _const_spec((None, None, 1, d), (layer, 2, 0, 0)),
                  _const_spec((None, d, d), (layer, 0, 0)),
                  pl.BlockSpec((1, m, 2 * d), lambda i: (i // tiles_per_seq, 0, 0)),
                  _const_spec((None, d, d), (layer, 0, 0)),
                  _const_spec((None, None, 1, d), (layer, 2, 0, 0))],
        out_specs=row_spec(d),
        scratch_shapes=[pltpu.VMEM((2 * len(DILATED_PATTERNS) * ATTN_WIDTH // LANES, ROW_TILE, LANES), F32)],
        compiler_params=_params(1),
        name="mix_out_xattn",
    )(h, a, bo, *branch_outs, w_out, npost, npre, wq, kv, wo, npost)


def _memkv_body(mem_ref, g_ref, w_ref, kv_ref):
    mn = _rms(mem_ref[0], g_ref[...]).astype(BF16)
    kv_ref[0] = _dot(mn, w_ref[...]).astype(BF16)


def _memkv(mem, g, wkv, layer):
    b, m, d = mem.shape
    return pl.pallas_call(
        _memkv_body,
        out_shape=jax.ShapeDtypeStruct((b, m, 2 * d), BF16),
        grid=(b,),
        in_specs=[pl.BlockSpec((1, m, d), lambda i: (i, 0, 0)), _const_spec((None, 1, d), (layer, 0, 0)),
                  _const_spec((None, d, 2 * d), (layer, 0, 0))],
        out_specs=pl.BlockSpec((1, m, 2 * d), lambda i: (i, 0, 0)),
        compiler_params=_params(1),
        name="mem_kv",
    )(mem, g, wkv)


def kernel(x, mem, norm_pre, norm_post, ffn_wi, ffn_wo, mix_w_in, mix_w_out, sgu_ln_g, sgu_w, sgu_b, ssm_conv_w, ssm_conv_b, ssm_dt_bias, ssm_a_log, ssm_d, ssm_norm_g, rel_bias, mem_norm_g, xattn_wq, xattn_wkv, xattn_wo):
    b, s, d = x.shape
    depth = norm_pre.shape[0]
    assert s % ATTN_TILE == 0 and s % ROW_TILE == 0 and s % SSD_TILE == 0
    assert all(s % (dil * CHUNK) == 0 for _, dil in DILATED_PATTERNS)

    npre = norm_pre[:, :, None, :]
    npost = norm_post[:, :, None, :]
    wi = ffn_wi.astype(BF16)
    wo = ffn_wo.astype(BF16)
    w_main = jnp.concatenate([mix_w_in[..., :OFF_DT], mix_w_in[..., OFF_QKV:]], axis=-1).astype(BF16)
    w_dt = jnp.repeat(mix_w_in[..., OFF_DT:OFF_QKV], HEAD_DIM, axis=-1).astype(BF16)
    w_out = mix_w_out.astype(BF16)
    lng = sgu_ln_g[:, None, :]
    sgu_b_e = jnp.repeat(jnp.swapaxes(sgu_b, 1, 2), HEAD_DIM, axis=-1)
    per_head = lambda p: jnp.repeat(p, HEAD_DIM, axis=-1)[:, None, :]
    dtb, alog, dskip = per_head(ssm_dt_bias), per_head(ssm_a_log), per_head(ssm_d)
    conv_b = ssm_conv_b[:, None, :]
    norm_g = ssm_norm_g[:, None, :]
    mem_g = mem_norm_g[:, None, :]
    wq, wkv, wxo = xattn_wq.astype(BF16), xattn_wkv.astype(BF16), xattn_wo.astype(BF16)
    biases = _bias_tables(rel_bias)

    h = x.reshape(b * s, d)
    for l in range(depth):
        h = _ffn(h, npre, wi, wo, npost, l, 0)
        a, zxd, bc, *kvq = _inproj(h, npre, w_main, w_dt, lng, sgu_w, sgu_b_e, ssm_conv_w, conv_b, dtb, l,
                                   s // ROW_TILE)
        seq = lambda t: t.reshape(b, s, t.shape[-1])
        bo = _ssd(seq(zxd), seq(bc), alog, dskip, norm_g, l)
        branch_outs = []
        for branch, t in enumerate(kvq):
            ol = _attn_branch(t.reshape(3, b, -1, t.shape[-1]), biases, branch)
            branch_outs.append(ol.reshape(2, -1, ol.shape[-1]))
        mkv = _memkv(mem, mem_g, wkv, l)
        h = _mix_xattn(h, a, bo.reshape(b * s, SSM_INNER), branch_outs, w_out, npost, npre, wq, mkv, wxo, l,
                       s // ROW_TILE)
        h = _ffn(h, npre, wi, wo, npost, l, 1)
    return h.reshape(b, s, d)
```

```python
import functools
import math

import jax
import jax.numpy as jnp
from jax import lax
from jax.experimental import pallas as pl
from jax.experimental.pallas import tpu as pltpu

F32 = jnp.float32
BF16 = jnp.bfloat16
EPS = 1e-6
LOG2E = math.log2(math.e)

LANES = 128
MXU_N = 256
HEAD_DIM = 64
CHUNK = 128
SGU_WIDTH = 256
SGU_GROUPS = SGU_WIDTH // HEAD_DIM
SSM_INNER = 512
SSM_HEADS = SSM_INNER // HEAD_DIM
SSM_GROUPS = 2
SSM_STATE = 128
SSM_CONV = 4
SSM_CONV_CH = SSM_INNER + 2 * SSM_GROUPS * SSM_STATE
GROUP_W = SSM_INNER // SSM_GROUPS
ATTN_WIDTH = 256
ATTN_HEADS = ATTN_WIDTH // HEAD_DIM
DILATED_PATTERNS = ((128, 1), (512, 4), (2048, 16))
REL_BUCKETS = 32
REL_MAX_DIST = 2048
XATTN_HEADS = 4

OFF_Z = 2 * SGU_WIDTH
OFF_XBC = OFF_Z + SSM_INNER
OFF_DT = OFF_XBC + SSM_CONV_CH
OFF_QKV = OFF_DT + SSM_HEADS
W_MAIN = OFF_DT + 3 * ATTN_WIDTH

ROW_TILE = 512
FFN_ROW_TILE = 2048
FF_TILE = 256
SSD_TILE = 1024
ATTN_TILE = 2048
CONV_PAD = 8
VMEM_LIMIT = 56 * 1024 * 1024


def _params(n_axes):
    return pltpu.CompilerParams(dimension_semantics=("arbitrary",) * n_axes,
                                vmem_limit_bytes=VMEM_LIMIT)


def _rms(x, g):
    return x * lax.rsqrt(jnp.mean(x * x, axis=-1, keepdims=True) + EPS) * g


def _silu(x):
    return x * jax.nn.sigmoid(x)


def _dot(a, b):
    return jnp.dot(a, b, preferred_element_type=F32)


def _dot_nt(a, b):
    return lax.dot_general(a, b, (((1,), (1,)), ((), ())), preferred_element_type=F32)


def _const_spec(shape, index):
    return pl.BlockSpec(shape, lambda *_: index, pipeline_mode=pl.Buffered(1))


def _ffn_body(h_ref, gpre_ref, wi_ref, wo_ref, gpost_ref, o_ref, *, d_ff):
    acc_ref = o_ref
    for r0 in range(0, FFN_ROW_TILE, ROW_TILE):
        rows = slice(r0, r0 + ROW_TILE)
        h = h_ref[rows, :]
        hn = _rms(h, gpre_ref[...]).astype(BF16)
        for c in range(d_ff // FF_TILE):
            lo = c * FF_TILE
            g = _dot(hn, wi_ref[:, lo:lo + FF_TILE])
            u = _dot(hn, wi_ref[:, d_ff + lo:d_ff + lo + FF_TILE])
            y = _dot((_silu(g) * u).astype(BF16), wo_ref[lo:lo + FF_TILE, :])
            if c == 0:
                acc_ref[rows, :] = y
            else:
                acc_ref[rows, :] += y
        o_ref[rows, :] = h + 0.5 * _rms(acc_ref[rows, :], gpost_ref[...])


def _ffn(h, gpre, wi, wo, gpost, layer, which):
    rows, d = h.shape
    d_ff = wo.shape[2]
    return pl.pallas_call(
        functools.partial(_ffn_body, d_ff=d_ff),
        out_shape=jax.ShapeDtypeStruct((rows, d), F32),
        grid=(rows // FFN_ROW_TILE,),
        in_specs=[
            pl.BlockSpec((FFN_ROW_TILE, d), lambda i: (i, 0)),
            _const_spec((None, None, 1, d), (layer, 3 * which, 0, 0)),
            _const_spec((None, None, d, 2 * d_ff), (layer, which, 0, 0)),
            _const_spec((None, None, d_ff, d), (layer, which, 0, 0)),
            _const_spec((None, None, 1, d), (layer, 3 * which, 0, 0)),
        ],
        out_specs=pl.BlockSpec((FFN_ROW_TILE, d), lambda i: (i, 0)),
        compiler_params=_params(1),
        name="ffn",
    )(h, gpre, wi, wo, gpost)


def _inproj_body(h_ref, gpre_ref, w_ref, wdt_ref, lng_ref, sw_ref, sb_ref, cw_ref, cb_ref, dtb_ref,
                 a_ref, zxd_ref, bc_ref, *rest, tiles_per_seq):
    kvq_refs, (slab_ref, ext_ref) = rest[:-2], rest[-2:]

    first = pl.program_id(0) % tiles_per_seq == 0

    @pl.when(first)
    def _():
        ext_ref[0:CONV_PAD, :] = jnp.zeros((CONV_PAD, SSM_CONV_CH), F32)

    @pl.when(jnp.logical_not(first))
    def _():
        ext_ref[0:CONV_PAD, :] = ext_ref[ROW_TILE:ROW_TILE + CONV_PAD, :]

    hn = _rms(h_ref[...], gpre_ref[...]).astype(BF16)

    def proj(lo, hi):
        return _dot(hn, w_ref[:, lo:hi])

    def stage_xbc(g):
        lo = g * MXU_N
        ext_ref[CONV_PAD:CONV_PAD + ROW_TILE, lo:lo + MXU_N] = proj(OFF_XBC + lo, OFF_XBC + lo + MXU_N)

    def conv_chunk(c, lo, hi, out_ref, out_lo):
        r0 = c * CHUNK
        for l0 in range(lo, hi, LANES):
            ls = slice(l0, l0 + LANES)
            win = ext_ref[r0:r0 + CONV_PAD + CHUNK, ls]
            acc = cb_ref[:, ls] + win[CONV_PAD:] * cw_ref[SSM_CONV - 1:SSM_CONV, ls]
            for back in range(1, SSM_CONV):
                tap = SSM_CONV - 1 - back
                acc = acc + pltpu.roll(win, back, 0)[CONV_PAD:] * cw_ref[tap:tap + 1, ls]
            o0 = out_lo + l0 - lo
            out_ref[r0:r0 + CHUNK, o0:o0 + LANES] = _silu(acc).astype(out_ref.dtype)

    conv_x = lambda c: conv_chunk(c, 0, SSM_INNER, zxd_ref, SSM_INNER)
    conv_bc = lambda c: conv_chunk(c, SSM_INNER, SSM_CONV_CH, bc_ref, 0)

    def dt_half(g):
        ls = slice(g * MXU_N, (g + 1) * MXU_N)
        zxd_ref[:, 2 * SSM_INNER + g * MXU_N:2 * SSM_INNER + (g + 1) * MXU_N] = (
            _softplus(_dot(hn, wdt_ref[:, ls]) + dtb_ref[:, ls]))

    uv = proj(0, OFF_Z)
    stage_xbc(0)
    stage_xbc(1)

    uv = jax.nn.gelu(uv)
    u = uv[:, :SGU_WIDTH]
    v = uv[:, SGU_WIDTH:]
    mu = jnp.mean(v, axis=-1, keepdims=True)
    var = jnp.mean(jnp.square(v - mu), axis=-1, keepdims=True)
    vn = ((v - mu) * lax.rsqrt(var + EPS) * lng_ref[...]).astype(BF16)
    causal = (lax.broadcasted_iota(jnp.int32, (CHUNK, CHUNK), 0)
              >= lax.broadcasted_iota(jnp.int32, (CHUNK, CHUNK), 1))
    lane_group = lax.broadcasted_iota(jnp.int32, (1, SGU_WIDTH), 1) // HEAD_DIM
    w_cat = jnp.concatenate([jnp.where(causal, sw_ref[g], 0.0).astype(BF16) for g in range(SGU_GROUPS)], axis=1)
    for c in range(ROW_TILE // CHUNK):
        r0 = c * CHUNK
        vc = vn[r0:r0 + CHUNK]
        v_blocks = [jnp.where(lane_group == g, vc, jnp.zeros_like(vc)) for g in range(SGU_GROUPS)]
        s = sb_ref[...] + _dot(w_cat, jnp.concatenate(v_blocks, axis=0))
        a_ref[r0:r0 + CHUNK, :] = u[r0:r0 + CHUNK] * s

    stage_xbc(2)
    conv_x(0)
    stage_xbc(3)
    conv_x(1)

    n_slab = ATTN_WIDTH // LANES
    for t, vector_step in enumerate((lambda: conv_x(2), lambda: conv_x(3), lambda: conv_bc(0))):
        val = proj(OFF_DT + t * ATTN_WIDTH, OFF_DT + (t + 1) * ATTN_WIDTH)
        if t == 0:
            val = val * (HEAD_DIM ** -0.5 * LOG2E)
        for sl in range(n_slab):
            slab_ref[t * n_slab + sl] = val[:, sl * LANES:(sl + 1) * LANES]
        plane = (t + 2) % 3
        for bi, (_, dil) in enumerate(DILATED_PATTERNS):
            out = kvq_refs[bi]
            if dil == 1:
                out[plane] = val.astype(BF16)
                continue
            for r in range(dil):
                for sl in range(n_slab):
                    lo = r * ATTN_WIDTH + sl * LANES
                    out[plane, :, lo:lo + LANES] = (
                        slab_ref[t * n_slab + sl, pl.ds(r, ROW_TILE // dil, stride=dil), :].astype(BF16))
        vector_step()

    dt_half(0)
    conv_bc(1)
    dt_half(1)
    conv_bc(2)
    zxd_ref[:, 0:MXU_N] = proj(OFF_Z, OFF_Z + MXU_N)
    conv_bc(3)
    zxd_ref[:, MXU_N:SSM_INNER] = proj(OFF_Z + MXU_N, OFF_XBC)


def _inproj(h, gpre, w_main, w_dt, lng, sgu_w, sgu_b, conv_w, conv_b, dtb, layer, tiles_per_seq):
    rows, d = h.shape
    row_spec = lambda w: pl.BlockSpec((ROW_TILE, w), lambda i: (i, 0))
    vec = lambda w: _const_spec((None, 1, w), (layer, 0, 0))
    outs = ((SGU_WIDTH, F32), (3 * SSM_INNER, F32), (SSM_CONV_CH - SSM_INNER, BF16))
    out_shape = [jax.ShapeDtypeStruct((rows, w), t) for w, t in outs]
    out_specs = [row_spec(w) for w, _ in outs]
    for _, dil in DILATED_PATTERNS:
        out_shape.append(jax.ShapeDtypeStruct((3, rows // dil, ATTN_WIDTH * dil), BF16))
        out_specs.append(pl.BlockSpec((3, ROW_TILE // dil, ATTN_WIDTH * dil), lambda i: (0, i, 0)))
    return pl.pallas_call(
        functools.partial(_inproj_body, tiles_per_seq=tiles_per_seq),
        out_shape=out_shape,
        grid=(rows // ROW_TILE,),
        in_specs=[
            row_spec(d),
            _const_spec((None, None, 1, d), (layer, 1, 0, 0)),
            _const_spec((None, d, W_MAIN), (layer, 0, 0)),
            _const_spec((None, d, SSM_INNER), (layer, 0, 0)),
            vec(SGU_WIDTH),
            _const_spec((None, SGU_GROUPS, CHUNK, CHUNK), (layer, 0, 0, 0)),
            _const_spec((None, CHUNK, SGU_WIDTH), (layer, 0, 0)),
            _const_spec((None, SSM_CONV, SSM_CONV_CH), (layer, 0, 0)),
            vec(SSM_CONV_CH), vec(SSM_INNER),
        ],
        out_specs=out_specs,
        scratch_shapes=[pltpu.VMEM((3 * ATTN_WIDTH // LANES, ROW_TILE, LANES), F32),
                        pltpu.VMEM((CONV_PAD + ROW_TILE, SSM_CONV_CH), F32)],
        compiler_params=_params(1),
        name="inproj_sgu",
    )(h, gpre, w_main, w_dt, lng, sgu_w, sgu_b, conv_w, conv_b, dtb)


def _softplus(x):
    return jnp.maximum(x, 0.0) + jnp.log1p(jnp.exp(-jnp.abs(x)))


def _cumsum_rows(tril, a):
    hi = a.astype(BF16)
    r1 = a - hi.astype(F32)
    mid = r1.astype(BF16)
    lo = (r1 - mid.astype(F32)).astype(BF16)
    return _dot(tril, hi) + _dot(tril, mid) + _dot(tril, lo)


def _ssd_body(zxd_ref, bc_ref, alog_ref, d_ref, ng_ref, o_ref, hs_ref):
    @pl.when(pl.program_id(1) == 0)
    def _():
        hs_ref[...] = jnp.zeros_like(hs_ref)

    row = lax.broadcasted_iota(jnp.int32, (CHUNK, CHUNK), 0)
    col = lax.broadcasted_iota(jnp.int32, (CHUNK, CHUNK), 1)
    causal = row >= col
    tril = jnp.where(causal, 1.0, 0.0).astype(BF16)
    low_half = col < HEAD_DIM
    lane_head = lax.broadcasted_iota(jnp.int32, (1, GROUP_W), 1) // HEAD_DIM
    a_neg = -jnp.exp(alog_ref[...]) * LOG2E
    c_off = SSM_GROUPS * SSM_STATE

    for c in range(SSD_TILE // CHUNK):
        r0 = c * CHUNK
        xs = zxd_ref[0, r0:r0 + CHUNK, SSM_INNER:2 * SSM_INNER]
        dt = zxd_ref[0, r0:r0 + CHUNK, 2 * SSM_INNER:3 * SSM_INNER]
        acs = _cumsum_rows(tril, dt * a_neg)
        acs_last = acs[CHUNK - 1:CHUNK, :]
        eacs = jnp.exp2(acs)
        x = xs * dt
        xb = x.astype(BF16)
        xd = (x * jnp.exp2(acs_last - acs)).astype(BF16)
        for g in range(SSM_GROUPS):
            gl = slice(g * GROUP_W, (g + 1) * GROUP_W)
            bg = bc_ref[0, r0:r0 + CHUNK, g * SSM_STATE:(g + 1) * SSM_STATE]
            cg = bc_ref[0, r0:r0 + CHUNK, c_off + g * SSM_STATE:c_off + (g + 1) * SSM_STATE]
            cbm = _dot_nt(cg, bg)
            xg = xb[:, gl]
            weights = []
            for e in range(SSM_HEADS // SSM_GROUPS):
                head = g * (SSM_HEADS // SSM_GROUPS) + e
                pair = acs[:, (head // 2) * 2 * HEAD_DIM:(head // 2 + 1) * 2 * HEAD_DIM]
                swapped = pltpu.roll(pair, HEAD_DIM, 1)
                acs_col = jnp.where(low_half, pair, swapped) if head % 2 == 0 else jnp.where(low_half, swapped, pair)
                seg = acs_col - acs_col.T
                decay = jnp.exp2(jnp.where(causal, seg, -jnp.inf))
                weights.append((cbm * decay).astype(BF16))
            x_blocks = [jnp.where(lane_head == e, xg, jnp.zeros_like(xg)) for e in range(SSM_HEADS // SSM_GROUPS)]
            yg = _dot(jnp.concatenate(weights, axis=1), jnp.concatenate(x_blocks, axis=0))
            h_prev = hs_ref[g]
            y_off = _dot(cg, h_prev.astype(BF16)) * eacs[:, gl]
            hs_ref[g] = h_prev * eacs[CHUNK - 1:CHUNK, gl] + _dot(bg.astype(F32).T.astype(BF16), xd[:, gl])
            y = yg + y_off + d_ref[:, gl] * xs[:, gl]
            y = y * _silu(zxd_ref[0, r0:r0 + CHUNK, gl])
            o_ref[0, r0:r0 + CHUNK, gl] = _rms(y, ng_ref[:, gl])


def _ssd(zxd, bc, alog, dskip, norm_g, layer):
    b, s, _ = zxd.shape
    tile = lambda w: pl.BlockSpec((1, SSD_TILE, w), lambda i, j: (i, j, 0))
    vec = lambda w: _const_spec((None, 1, w), (layer, 0, 0))
    return pl.pallas_call(
        _ssd_body,
        out_shape=jax.ShapeDtypeStruct((b, s, SSM_INNER), F32),
        grid=(b, s // SSD_TILE),
        in_specs=[tile(3 * SSM_INNER), tile(SSM_CONV_CH - SSM_INNER), vec(SSM_INNER), vec(SSM_INNER), vec(SSM_INNER)],
        out_specs=tile(SSM_INNER),
        scratch_shapes=[pltpu.VMEM((SSM_GROUPS, SSM_STATE, GROUP_W), F32)],
        compiler_params=_params(2),
        name="ssd",
    )(zxd, bc, alog, dskip, norm_g)


def _attn_body(q_ref, kvp_ref, kvc_ref, bias_ref, ol_ref, *, tq, ncls):
    n = pl.program_id(2)
    nblk = tq // CHUNK
    w = ATTN_WIDTH
    lane_head = lax.broadcasted_iota(jnp.int32, (1, w), 1) // HEAD_DIM
    first_table = jnp.where(n == 0, 1, 0)
    for c in range(ncls):
        cl = slice(c * w, (c + 1) * w)
        for j in range(nblk):
            rows = slice(j * CHUNK, (j + 1) * CHUNK)
            q = q_ref[0, rows, cl]
            lhs = jnp.concatenate([jnp.where(lane_head == h, q, jnp.zeros_like(q)) for h in range(ATTN_HEADS)], axis=0)
            if j == 0:
                kk, vv = (jnp.concatenate([kvp_ref[t, 0, :, cl], kvc_ref[t, 0, 0:CHUNK, cl]], axis=0) for t in (0, 1))
            else:
                kk, vv = (kvc_ref[t, 0, (j - 1) * CHUNK:(j + 1) * CHUNK, cl] for t in (0, 1))
            logits = _dot_nt(lhs, kk) + bias_ref[first_table if j == 0 else 0]
            m = jnp.max(logits, axis=-1, keepdims=True)
            p = jnp.exp2(logits - m)
            s = jnp.sum(p, axis=-1, keepdims=True)
            pv = _dot(p.astype(BF16), vv) * (1.0 / s)
            lse = m + jnp.log2(s)
            o_acc = pv[0:CHUNK]
            lse_acc = jnp.broadcast_to(lse[0:CHUNK], (CHUNK, w))
            for h in range(1, ATTN_HEADS):
                hr = slice(h * CHUNK, (h + 1) * CHUNK)
                o_acc = jnp.where(lane_head == h, pv[hr], o_acc)
                lse_acc = jnp.where(lane_head == h, lse[hr], lse_acc)
            ol_ref[0, 0, rows, cl] = o_acc
            ol_ref[1, 0, rows, cl] = lse_acc


def _attn_branch(kvq, biases, branch):
    dil = DILATED_PATTERNS[branch][1]
    w = ATTN_WIDTH
    _, b, cls_len, _ = kvq.shape
    tq = min(ATTN_TILE, cls_len)
    ncls = min(dil, ATTN_TILE // tq)
    prev_block = lambda n: jnp.maximum(n * (tq // CHUNK) - 1, 0)
    return pl.pallas_call(
        functools.partial(_attn_body, tq=tq, ncls=ncls),
        out_shape=jax.ShapeDtypeStruct((2, b, cls_len, dil * w), F32),
        grid=(b, dil // ncls, cls_len // tq),
        in_specs=[pl.BlockSpec((None, 1, tq, ncls * w), lambda i, r, n: (2, i, n, r)),
                  pl.BlockSpec((2, 1, CHUNK, ncls * w), lambda i, r, n: (0, i, prev_block(n), r)),
                  pl.BlockSpec((2, 1, tq, ncls * w), lambda i, r, n: (0, i, n, r)),
                  _const_spec((None, 2, ATTN_HEADS * CHUNK, 2 * CHUNK), (branch, 0, 0, 0))],
        out_specs=pl.BlockSpec((2, 1, tq, ncls * w), lambda i, r, n: (0, i, n, r)),
        compiler_params=_params(3),
        name=f"dilated_attn_{dil}",
    )(kvq, kvq, kvq, biases)


def _t5_bucket(dist):
    max_exact = REL_BUCKETS // 2
    d = jnp.maximum(dist, 1).astype(F32)
    large = max_exact + (jnp.log(d / max_exact) / math.log(REL_MAX_DIST / max_exact)
                         * (REL_BUCKETS - max_exact)).astype(jnp.int32)
    large = jnp.minimum(large, REL_BUCKETS - 1)
    return jnp.where(dist < max_exact, dist, large)


def _bias_body(rel_ref, span_ref, bucket_ref, o_ref):
    bucket = bucket_ref[0]
    row = lax.broadcasted_iota(jnp.int32, (CHUNK, 2 * CHUNK), 0)
    col = lax.broadcasted_iota(jnp.int32, (CHUNK, 2 * CHUNK), 1)
    dist = row + CHUNK - col
    band = (dist >= 0) & (dist <= span_ref[pl.program_id(0)])
    band_cur = band & (col >= CHUNK)
    for h in range(ATTN_HEADS):
        bias = jnp.zeros((CHUNK, 2 * CHUNK), F32)
        for b in range(REL_BUCKETS):
            bias = jnp.where(bucket == b, rel_ref[b, h], bias)
        bias = bias * LOG2E
        o_ref[0, 0, h * CHUNK:(h + 1) * CHUNK, :] = jnp.where(band, bias, -jnp.inf)
        o_ref[0, 1, h * CHUNK:(h + 1) * CHUNK, :] = jnp.where(band_cur, bias, -jnp.inf)


def _bias_tables(rel_bias):
    assert all(window // dil <= CHUNK for window, dil in DILATED_PATTERNS)
    dist = jnp.arange(CHUNK)[:, None] + CHUNK - jnp.arange(2 * CHUNK)[None, :]
    buckets = jnp.stack([_t5_bucket(jnp.maximum(dist, 0) * dil) for _, dil in DILATED_PATTERNS]).astype(jnp.int32)
    spans = jnp.array([window // dil for window, dil in DILATED_PATTERNS], jnp.int32)
    nb = len(DILATED_PATTERNS)
    return pl.pallas_call(
        _bias_body,
        out_shape=jax.ShapeDtypeStruct((nb, 2, ATTN_HEADS * CHUNK, 2 * CHUNK), F32),
        grid=(nb,),
        in_specs=[pl.BlockSpec(memory_space=pltpu.SMEM), pl.BlockSpec(memory_space=pltpu.SMEM),
                  pl.BlockSpec((1, CHUNK, 2 * CHUNK), lambda i: (i, 0, 0))],
        out_specs=pl.BlockSpec((1, 2, ATTN_HEADS * CHUNK, 2 * CHUNK), lambda i: (i, 0, 0, 0)),
        compiler_params=_params(1),
        name="rel_bias_tables",
    )(rel_bias, spans, buckets)


def _mix_xattn_body(h_ref, a_ref, b_ref, *rest):
    nb = len(DILATED_PATTERNS)
    branch_refs = rest[:nb]
    w_ref, gpost_ref, xpre_ref, wq_ref, kv_ref, wo_ref, xpost_ref, out_ref, nat_ref = rest[nb:]
    n_slab = ATTN_WIDTH // LANES
    off_b = SGU_WIDTH
    off_c = SGU_WIDTH + SSM_INNER
    d = h_ref.shape[-1]
    hd = d // XATTN_HEADS

    def token_order(bi, plane):
        ref, dil = branch_refs[bi], DILATED_PATTERNS[bi][1]
        if dil == 1:
            return ref[plane]
        slot = (plane * nb + bi) * n_slab
        for r in range(dil):
            for sl in range(n_slab):
                lo = r * ATTN_WIDTH + sl * LANES
                nat_ref[slot + sl, pl.ds(r, ROW_TILE // dil, stride=dil), :] = ref[plane, :, lo:lo + LANES]
        return jnp.concatenate([nat_ref[slot + sl] for sl in range(n_slab)], axis=-1)

    o1, o2, o3 = (token_order(bi, 0) for bi in range(nb))
    l1, l2, l3 = (token_order(bi, 1) for bi in range(nb))
    m = jnp.maximum(jnp.maximum(l1, l2), l3)
    e1, e2, e3 = jnp.exp2(l1 - m), jnp.exp2(l2 - m), jnp.exp2(l3 - m)
    c = (e1 * o1 + e2 * o2 + e3 * o3) * (1.0 / (e1 + e2 + e3))
    y = (_dot(a_ref[...].astype(BF16), w_ref[0:off_b, :])
         + _dot(b_ref[...].astype(BF16), w_ref[off_b:off_c, :])
         + _dot(c.astype(BF16), w_ref[off_c:off_c + ATTN_WIDTH, :]))
    h = h_ref[...] + _rms(y, gpost_ref[...])

    hn = _rms(h, xpre_ref[...]).astype(BF16)
    q = (_dot(hn, wq_ref[...]) * (hd ** -0.5 * LOG2E)).astype(BF16)
    heads = []
    for i in range(XATTN_HEADS):
        hs = slice(i * hd, (i + 1) * hd)
        logits = _dot_nt(q[:, hs], kv_ref[0, :, hs])
        m = jnp.max(logits, axis=-1, keepdims=True)
        p = jnp.exp2(logits - m)
        s = jnp.sum(p, axis=-1, keepdims=True)
        heads.append((_dot(p.astype(BF16), kv_ref[0, :, d + i * hd:d + (i + 1) * hd]) * (1.0 / s)).astype(BF16))
    y = _dot(jnp.concatenate(heads, axis=-1), wo_ref[...])
    out_ref[...] = h + _rms(y, xpost_ref[...])


def _mix_xattn(h, a, bo, branch_outs, w_out, npost, npre, wq, kv, wo, layer, tiles_per_seq):
    rows, d = h.shape
    m = kv.shape[1]
    row_spec = lambda w: pl.BlockSpec((ROW_TILE, w), lambda i: (i, 0))
    branch = [pl.BlockSpec((2, ROW_TILE // dil, ATTN_WIDTH * dil), lambda i: (0, i, 0)) for _, dil in DILATED_PATTERNS]
    return pl.pallas_call(
        _mix_xattn_body,
        out_shape=jax.ShapeDtypeStruct((rows, d), F32),
        grid=(rows // ROW_TILE,),
        in_specs=[row_spec(d), row_spec(SGU_WIDTH), row_spec(SSM_INNER), *branch,
                  _const_spec((None, d, d), (layer, 0, 0)),
                  _const_spec((None, None, 1, d), (layer, 1, 0, 0)),
                  _const_spec((None, None, 1, d), (layer, 2, 0, 0)),
                  _const_spec((None, d, d), (layer, 0, 0)),
                  pl.BlockSpec((1, m, 2 * d), lambda i: (i // tiles_per_seq, 0, 0)),
                  _const_spec((None, d, d), (layer, 0, 0)),
                  _const_spec((None, None, 1, d), (layer, 2, 0, 0))],
        out_specs=row_spec(d),
        scratch_shapes=[pltpu.VMEM((2 * len(DILATED_PATTERNS) * ATTN_WIDTH // LANES, ROW_TILE, LANES), F32)],
        compiler_params=_params(1),
        name="mix_out_xattn",
    )(h, a, bo, *branch_outs, w_out, npost, npre, wq, kv, wo, npost)


def _memkv_body(mem_ref, g_ref, w_ref, kv_ref):
    mn = _rms(mem_ref[0], g_ref[...]).astype(BF16)
    kv_ref[0] = _dot(mn, w_ref[...]).astype(BF16)


def _memkv(mem, g, wkv, layer):
    b, m, d = mem.shape
    return pl.pallas_call(
        _memkv_body,
        out_shape=jax.ShapeDtypeStruct((b, m, 2 * d), BF16),
        grid=(b,),
        in_specs=[pl.BlockSpec((1, m, d), lambda i: (i, 0, 0)), _const_spec((None, 1, d), (layer, 0, 0)),
                  _const_spec((None, d, 2 * d), (layer, 0, 0))],
        out_specs=pl.BlockSpec((1, m, 2 * d), lambda i: (i, 0, 0)),
        compiler_params=_params(1),
        name="mem_kv",
    )(mem, g, wkv)


def kernel(x, mem, norm_pre, norm_post, ffn_wi, ffn_wo, mix_w_in, mix_w_out, sgu_ln_g, sgu_w, sgu_b, ssm_conv_w, ssm_conv_b, ssm_dt_bias, ssm_a_log, ssm_d, ssm_norm_g, rel_bias, mem_norm_g, xattn_wq, xattn_wkv, xattn_wo):
    b, s, d = x.shape
    depth = norm_pre.shape[0]
    assert s % ATTN_TILE == 0 and s % ROW_TILE == 0 and s % SSD_TILE == 0
    assert all(s % (dil * CHUNK) == 0 for _, dil in DILATED_PATTERNS)

    npre = norm_pre[:, :, None, :]
    npost = norm_post[:, :, None, :]
    wi = ffn_wi.astype(BF16)
    wo = ffn_wo.astype(BF16)
    w_main = jnp.concatenate([mix_w_in[..., :OFF_DT], mix_w_in[..., OFF_QKV:]], axis=-1).astype(BF16)
    w_dt = jnp.repeat(mix_w_in[..., OFF_DT:OFF_QKV], HEAD_DIM, axis=-1).astype(BF16)
    w_out = mix_w_out.astype(BF16)
    lng = sgu_ln_g[:, None, :]
    sgu_b_e = jnp.repeat(jnp.swapaxes(sgu_b, 1, 2), HEAD_DIM, axis=-1)
    per_head = lambda p: jnp.repeat(p, HEAD_DIM, axis=-1)[:, None, :]
    dtb, alog, dskip = per_head(ssm_dt_bias), per_head(ssm_a_log), per_head(ssm_d)
    conv_b = ssm_conv_b[:, None, :]
    norm_g = ssm_norm_g[:, None, :]
    mem_g = mem_norm_g[:, None, :]
    wq, wkv, wxo = xattn_wq.astype(BF16), xattn_wkv.astype(BF16), xattn_wo.astype(BF16)
    biases = _bias_tables(rel_bias)

    h = x.reshape(b * s, d)
    for l in range(depth):
        h = _ffn(h, npre, wi, wo, npost, l, 0)
        a, zxd, bc, *kvq = _inproj(h, npre, w_main, w_dt, lng, sgu_w, sgu_b_e, ssm_conv_w, conv_b, dtb, l,
                                   s // ROW_TILE)
        seq = lambda t: t.reshape(b, s, t.shape[-1])
        bo = _ssd(seq(zxd), seq(bc), alog, dskip, norm_g, l)
        branch_outs = []
        for branch, t in enumerate(kvq):
            ol = _attn_branch(t.reshape(3, b, -1, t.shape[-1]), biases, branch)
            branch_outs.append(ol.reshape(2, -1, ol.shape[-1]))
        mkv = _memkv(mem, mem_g, wkv, l)
        h = _mix_xattn(h, a, bo.reshape(b * s, SSM_INNER), branch_outs, w_out, npost, npre, wq, mkv, wxo, l,
                       s // ROW_TILE)
        h = _ffn(h, npre, wi, wo, npost, l, 1)
    return h.reshape(b, s, d)
```

```python
import functools
import math

import jax
import jax.numpy as jnp
from jax import lax
from jax.experimental import pallas as pl
from jax.experimental.pallas import tpu as pltpu

F32 = jnp.float32
BF16 = jnp.bfloat16
EPS = 1e-6
LOG2E = math.log2(math.e)

LANES = 128
MXU_N = 256
HEAD_DIM = 64
CHUNK = 128
SGU_WIDTH = 256
SGU_GROUPS = SGU_WIDTH // HEAD_DIM
SSM_INNER = 512
SSM_HEADS = SSM_INNER // HEAD_DIM
SSM_GROUPS = 2
SSM_STATE = 128
SSM_CONV = 4
SSM_CONV_CH = SSM_INNER + 2 * SSM_GROUPS * SSM_STATE
GROUP_W = SSM_INNER // SSM_GROUPS
ATTN_WIDTH = 256
ATTN_HEADS = ATTN_WIDTH // HEAD_DIM
DILATED_PATTERNS = ((128, 1), (512, 4), (2048, 16))
REL_BUCKETS = 32
REL_MAX_DIST = 2048
XATTN_HEADS = 4

OFF_Z = 2 * SGU_WIDTH
OFF_XBC = OFF_Z + SSM_INNER
OFF_DT = OFF_XBC + SSM_CONV_CH
OFF_QKV = OFF_DT + SSM_HEADS
W_MAIN = OFF_DT + 3 * ATTN_WIDTH

ROW_TILE = 512
FFN_ROW_TILE = 1024
FF_TILE = 256
SSD_TILE = 1024
ATTN_TILE = 2048
CONV_PAD = 8
VMEM_LIMIT = 56 * 1024 * 1024


def _params(n_axes):
    return pltpu.CompilerParams(dimension_semantics=("arbitrary",) * n_axes,
                                vmem_limit_bytes=VMEM_LIMIT)


def _rms(x, g):
    return x * lax.rsqrt(jnp.mean(x * x, axis=-1, keepdims=True) + EPS) * g


def _silu(x):
    return x * jax.nn.sigmoid(x)


def _dot(a, b):
    return jnp.dot(a, b, preferred_element_type=F32)


def _dot_nt(a, b):
    return lax.dot_general(a, b, (((1,), (1,)), ((), ())), preferred_element_type=F32)


def _const_spec(shape, index):
    return pl.BlockSpec(shape, lambda *_: index, pipeline_mode=pl.Buffered(1))


def _ffn_body(h_ref, gpre_ref, wi_ref, wo_ref, gpost_ref, o_ref, acc_ref, *, d_ff):
    for r0 in range(0, FFN_ROW_TILE, ROW_TILE):
        rows = slice(r0, r0 + ROW_TILE)
        h = h_ref[rows, :]
        hn = _rms(h, gpre_ref[...]).astype(BF16)
        for c in range(d_ff // FF_TILE):
            lo = c * FF_TILE
            g = _dot(hn, wi_ref[:, lo:lo + FF_TILE])
            u = _dot(hn, wi_ref[:, d_ff + lo:d_ff + lo + FF_TILE])
            y = _dot((_silu(g) * u).astype(BF16), wo_ref[lo:lo + FF_TILE, :])
            if c == 0:
                acc_ref[rows, :] = y
            else:
                acc_ref[rows, :] += y
        o_ref[rows, :] = h + 0.5 * _rms(acc_ref[rows, :], gpost_ref[...])


def _ffn(h, gpre, wi, wo, gpost, layer, which):
    rows, d = h.shape
    d_ff = wo.shape[2]
    return pl.pallas_call(
        functools.partial(_ffn_body, d_ff=d_ff),
        out_shape=jax.ShapeDtypeStruct((rows, d), F32),
        grid=(rows // FFN_ROW_TILE,),
        in_specs=[
            pl.BlockSpec((FFN_ROW_TILE, d), lambda i: (i, 0)),
            _const_spec((None, None, 1, d), (layer, 3 * which, 0, 0)),
            _const_spec((None, None, d, 2 * d_ff), (layer, which, 0, 0)),
            _const_spec((None, None, d_ff, d), (layer, which, 0, 0)),
            _const_spec((None, None, 1, d), (layer, 3 * which, 0, 0)),
        ],
        out_specs=pl.BlockSpec((FFN_ROW_TILE, d), lambda i: (i, 0)),
        scratch_shapes=[pltpu.VMEM((FFN_ROW_TILE, d), F32)],
        compiler_params=_params(1),
        name="ffn",
    )(h, gpre, wi, wo, gpost)


def _inproj_body(h_ref, gpre_ref, w_ref, wdt_ref, lng_ref, sw_ref, sb_ref, cw_ref, cb_ref, dtb_ref,
                 a_ref, zxd_ref, bc_ref, *rest, tiles_per_seq):
    kvq_refs, (slab_ref, ext_ref) = rest[:-2], rest[-2:]

    first = pl.program_id(0) % tiles_per_seq == 0

    @pl.when(first)
    def _():
        ext_ref[0:CONV_PAD, :] = jnp.zeros((CONV_PAD, SSM_CONV_CH), F32)

    @pl.when(jnp.logical_not(first))
    def _():
        ext_ref[0:CONV_PAD, :] = ext_ref[ROW_TILE:ROW_TILE + CONV_PAD, :]

    hn = _rms(h_ref[...], gpre_ref[...]).astype(BF16)

    def proj(lo, hi):
        return _dot(hn, w_ref[:, lo:hi])

    def stage_xbc(g):
        lo = g * MXU_N
        ext_ref[CONV_PAD:CONV_PAD + ROW_TILE, lo:lo + MXU_N] = proj(OFF_XBC + lo, OFF_XBC + lo + MXU_N)

    def conv_chunk(c, lo, hi, out_ref, out_lo):
        r0 = c * CHUNK
        for l0 in range(lo, hi, LANES):
            ls = slice(l0, l0 + LANES)
            win = ext_ref[r0:r0 + CONV_PAD + CHUNK, ls]
            acc = cb_ref[:, ls] + win[CONV_PAD:] * cw_ref[SSM_CONV - 1:SSM_CONV, ls]
            for back in range(1, SSM_CONV):
                tap = SSM_CONV - 1 - back
                acc = acc + pltpu.roll(win, back, 0)[CONV_PAD:] * cw_ref[tap:tap + 1, ls]
            o0 = out_lo + l0 - lo
            out_ref[r0:r0 + CHUNK, o0:o0 + LANES] = _silu(acc).astype(out_ref.dtype)

    conv_x = lambda c: conv_chunk(c, 0, SSM_INNER, zxd_ref, SSM_INNER)
    conv_bc = lambda c: conv_chunk(c, SSM_INNER, SSM_CONV_CH, bc_ref, 0)

    def dt_half(g):
        ls = slice(g * MXU_N, (g + 1) * MXU_N)
        zxd_ref[:, 2 * SSM_INNER + g * MXU_N:2 * SSM_INNER + (g + 1) * MXU_N] = (
            _softplus(_dot(hn, wdt_ref[:, ls]) + dtb_ref[:, ls]))

    uv = proj(0, OFF_Z)
    stage_xbc(0)
    stage_xbc(1)

    uv = jax.nn.gelu(uv)
    u = uv[:, :SGU_WIDTH]
    v = uv[:, SGU_WIDTH:]
    mu = jnp.mean(v, axis=-1, keepdims=True)
    var = jnp.mean(jnp.square(v - mu), axis=-1, keepdims=True)
    vn = ((v - mu) * lax.rsqrt(var + EPS) * lng_ref[...]).astype(BF16)
    causal = (lax.broadcasted_iota(jnp.int32, (CHUNK, CHUNK), 0)
              >= lax.broadcasted_iota(jnp.int32, (CHUNK, CHUNK), 1))
    lane_group = lax.broadcasted_iota(jnp.int32, (1, SGU_WIDTH), 1) // HEAD_DIM
    w_cat = jnp.concatenate([jnp.where(causal, sw_ref[g], 0.0).astype(BF16) for g in range(SGU_GROUPS)], axis=1)
    for c in range(ROW_TILE // CHUNK):
        r0 = c * CHUNK
        vc = vn[r0:r0 + CHUNK]
        v_blocks = [jnp.where(lane_group == g, vc, jnp.zeros_like(vc)) for g in range(SGU_GROUPS)]
        s = sb_ref[...] + _dot(w_cat, jnp.concatenate(v_blocks, axis=0))
        a_ref[r0:r0 + CHUNK, :] = u[r0:r0 + CHUNK] * s

    stage_xbc(2)
    conv_x(0)
    stage_xbc(3)
    conv_x(1)

    n_slab = ATTN_WIDTH // LANES
    for t, vector_step in enumerate((lambda: conv_x(2), lambda: conv_x(3), lambda: conv_bc(0))):
        val = proj(OFF_DT + t * ATTN_WIDTH, OFF_DT + (t + 1) * ATTN_WIDTH)
        if t == 0:
            val = val * (HEAD_DIM ** -0.5 * LOG2E)
        for sl in range(n_slab):
            slab_ref[t * n_slab + sl] = val[:, sl * LANES:(sl + 1) * LANES]
        plane = (t + 2) % 3
        for bi, (_, dil) in enumerate(DILATED_PATTERNS):
            out = kvq_refs[bi]
            if dil == 1:
                out[plane] = val.astype(BF16)
                continue
            for r in range(dil):
                for sl in range(n_slab):
                    lo = r * ATTN_WIDTH + sl * LANES
                    out[plane, :, lo:lo + LANES] = (
                        slab_ref[t * n_slab + sl, pl.ds(r, ROW_TILE // dil, stride=dil), :].astype(BF16))
        vector_step()

    dt_half(0)
    conv_bc(1)
    dt_half(1)
    conv_bc(2)
    zxd_ref[:, 0:MXU_N] = proj(OFF_Z, OFF_Z + MXU_N)
    conv_bc(3)
    zxd_ref[:, MXU_N:SSM_INNER] = proj(OFF_Z + MXU_N, OFF_XBC)


def _inproj(h, gpre, w_main, w_dt, lng, sgu_w, sgu_b, conv_w, conv_b, dtb, layer, tiles_per_seq):
    rows, d = h.shape
    row_spec = lambda w: pl.BlockSpec((ROW_TILE, w), lambda i: (i, 0))
    vec = lambda w: _const_spec((None, 1, w), (layer, 0, 0))
    outs = ((SGU_WIDTH, F32), (3 * SSM_INNER, F32), (SSM_CONV_CH - SSM_INNER, BF16))
    out_shape = [jax.ShapeDtypeStruct((rows, w), t) for w, t in outs]
    out_specs = [row_spec(w) for w, _ in outs]
    for _, dil in DILATED_PATTERNS:
        out_shape.append(jax.ShapeDtypeStruct((3, rows // dil, ATTN_WIDTH * dil), BF16))
        out_specs.append(pl.BlockSpec((3, ROW_TILE // dil, ATTN_WIDTH * dil), lambda i: (0, i, 0)))
    return pl.pallas_call(
        functools.partial(_inproj_body, tiles_per_seq=tiles_per_seq),
        out_shape=out_shape,
        grid=(rows // ROW_TILE,),
        in_specs=[
            row_spec(d),
            _const_spec((None, None, 1, d), (layer, 1, 0, 0)),
            _const_spec((None, d, W_MAIN), (layer, 0, 0)),
            _const_spec((None, d, SSM_INNER), (layer, 0, 0)),
            vec(SGU_WIDTH),
            _const_spec((None, SGU_GROUPS, CHUNK, CHUNK), (layer, 0, 0, 0)),
            _const_spec((None, CHUNK, SGU_WIDTH), (layer, 0, 0)),
            _const_spec((None, SSM_CONV, SSM_CONV_CH), (layer, 0, 0)),
            vec(SSM_CONV_CH), vec(SSM_INNER),
        ],
        out_specs=out_specs,
        scratch_shapes=[pltpu.VMEM((3 * ATTN_WIDTH // LANES, ROW_TILE, LANES), F32),
                        pltpu.VMEM((CONV_PAD + ROW_TILE, SSM_CONV_CH), F32)],
        compiler_params=_params(1),
        name="inproj_sgu",
    )(h, gpre, w_main, w_dt, lng, sgu_w, sgu_b, conv_w, conv_b, dtb)


def _softplus(x):
    return jnp.maximum(x, 0.0) + jnp.log1p(jnp.exp(-jnp.abs(x)))


def _cumsum_rows(tril, a):
    hi = a.astype(BF16)
    r1 = a - hi.astype(F32)
    mid = r1.astype(BF16)
    lo = (r1 - mid.astype(F32)).astype(BF16)
    return _dot(tril, hi) + _dot(tril, mid) + _dot(tril, lo)


def _ssd_body(zxd_ref, bc_ref, alog_ref, d_ref, ng_ref, o_ref, hs_ref):
    @pl.when(pl.program_id(1) == 0)
    def _():
        hs_ref[...] = jnp.zeros_like(hs_ref)

    row = lax.broadcasted_iota(jnp.int32, (CHUNK, CHUNK), 0)
    col = lax.broadcasted_iota(jnp.int32, (CHUNK, CHUNK), 1)
    causal = row >= col
    tril = jnp.where(causal, 1.0, 0.0).astype(BF16)
    low_half = col < HEAD_DIM
    lane_head = lax.broadcasted_iota(jnp.int32, (1, GROUP_W), 1) // HEAD_DIM
    a_neg = -jnp.exp(alog_ref[...]) * LOG2E
    c_off = SSM_GROUPS * SSM_STATE

    for c in range(SSD_TILE // CHUNK):
        r0 = c * CHUNK
        xs = zxd_ref[0, r0:r0 + CHUNK, SSM_INNER:2 * SSM_INNER]
        dt = zxd_ref[0, r0:r0 + CHUNK, 2 * SSM_INNER:3 * SSM_INNER]
        acs = _cumsum_rows(tril, dt * a_neg)
        acs_last = acs[CHUNK - 1:CHUNK, :]
        eacs = jnp.exp2(acs)
        x = xs * dt
        xb = x.astype(BF16)
        xd = (x * jnp.exp2(acs_last - acs)).astype(BF16)
        for g in range(SSM_GROUPS):
            gl = slice(g * GROUP_W, (g + 1) * GROUP_W)
            bg = bc_ref[0, r0:r0 + CHUNK, g * SSM_STATE:(g + 1) * SSM_STATE]
            cg = bc_ref[0, r0:r0 + CHUNK, c_off + g * SSM_STATE:c_off + (g + 1) * SSM_STATE]
            cbm = _dot_nt(cg, bg)
            xg = xb[:, gl]
            weights = []
            for e in range(SSM_HEADS // SSM_GROUPS):
                head = g * (SSM_HEADS // SSM_GROUPS) + e
                pair = acs[:, (head // 2) * 2 * HEAD_DIM:(head // 2 + 1) * 2 * HEAD_DIM]
                swapped = pltpu.roll(pair, HEAD_DIM, 1)
                acs_col = jnp.where(low_half, pair, swapped) if head % 2 == 0 else jnp.where(low_half, swapped, pair)
                seg = acs_col - acs_col.T
                decay = jnp.exp2(jnp.where(causal, seg, -jnp.inf))
                weights.append((cbm * decay).astype(BF16))
            x_blocks = [jnp.where(lane_head == e, xg, jnp.zeros_like(xg)) for e in range(SSM_HEADS // SSM_GROUPS)]
            yg = _dot(jnp.concatenate(weights, axis=1), jnp.concatenate(x_blocks, axis=0))
            h_prev = hs_ref[g]
            y_off = _dot(cg, h_prev.astype(BF16)) * eacs[:, gl]
            hs_ref[g] = h_prev * eacs[CHUNK - 1:CHUNK, gl] + _dot(bg.astype(F32).T.astype(BF16), xd[:, gl])
            y = yg + y_off + d_ref[:, gl] * xs[:, gl]
            y = y * _silu(zxd_ref[0, r0:r0 + CHUNK, gl])
            o_ref[0, r0:r0 + CHUNK, gl] = _rms(y, ng_ref[:, gl])


def _ssd(zxd, bc, alog, dskip, norm_g, layer):
    b, s, _ = zxd.shape
    tile = lambda w: pl.BlockSpec((1, SSD_TILE, w), lambda i, j: (i, j, 0))
    vec = lambda w: _const_spec((None, 1, w), (layer, 0, 0))
    return pl.pallas_call(
        _ssd_body,
        out_shape=jax.ShapeDtypeStruct((b, s, SSM_INNER), F32),
        grid=(b, s // SSD_TILE),
        in_specs=[tile(3 * SSM_INNER), tile(SSM_CONV_CH - SSM_INNER), vec(SSM_INNER), vec(SSM_INNER), vec(SSM_INNER)],
        out_specs=tile(SSM_INNER),
        scratch_shapes=[pltpu.VMEM((SSM_GROUPS, SSM_STATE, GROUP_W), F32)],
        compiler_params=_params(2),
        name="ssd",
    )(zxd, bc, alog, dskip, norm_g)


def _attn_body(q_ref, kvp_ref, kvc_ref, bias_ref, ol_ref, *, tq, ncls):
    n = pl.program_id(2)
    nblk = tq // CHUNK
    w = ATTN_WIDTH
    lane_head = lax.broadcasted_iota(jnp.int32, (1, w), 1) // HEAD_DIM
    first_table = jnp.where(n == 0, 1, 0)
    for c in range(ncls):
        cl = slice(c * w, (c + 1) * w)
        for j in range(nblk):
            rows = slice(j * CHUNK, (j + 1) * CHUNK)
            q = q_ref[0, rows, cl]
            lhs = jnp.concatenate([jnp.where(lane_head == h, q, jnp.zeros_like(q)) for h in range(ATTN_HEADS)], axis=0)
            if j == 0:
                kk, vv = (jnp.concatenate([kvp_ref[t, 0, :, cl], kvc_ref[t, 0, 0:CHUNK, cl]], axis=0) for t in (0, 1))
            else:
                kk, vv = (kvc_ref[t, 0, (j - 1) * CHUNK:(j + 1) * CHUNK, cl] for t in (0, 1))
            logits = _dot_nt(lhs, kk) + bias_ref[first_table if j == 0 else 0]
            m = jnp.max(logits, axis=-1, keepdims=True)
            p = jnp.exp2(logits - m)
            s = jnp.sum(p, axis=-1, keepdims=True)
            pv = _dot(p.astype(BF16), vv) * (1.0 / s)
            lse = m + jnp.log2(s)
            o_acc = pv[0:CHUNK]
            lse_acc = jnp.broadcast_to(lse[0:CHUNK], (CHUNK, w))
            for h in range(1, ATTN_HEADS):
                hr = slice(h * CHUNK, (h + 1) * CHUNK)
                o_acc = jnp.where(lane_head == h, pv[hr], o_acc)
                lse_acc = jnp.where(lane_head == h, lse[hr], lse_acc)
            ol_ref[0, 0, rows, cl] = o_acc
            ol_ref[1, 0, rows, cl] = lse_acc


def _attn_branch(kvq, biases, branch):
    dil = DILATED_PATTERNS[branch][1]
    w = ATTN_WIDTH
    _, b, cls_len, _ = kvq.shape
    tq = min(ATTN_TILE, cls_len)
    ncls = min(dil, ATTN_TILE // tq)
    prev_block = lambda n: jnp.maximum(n * (tq // CHUNK) - 1, 0)
    return pl.pallas_call(
        functools.partial(_attn_body, tq=tq, ncls=ncls),
        out_shape=jax.ShapeDtypeStruct((2, b, cls_len, dil * w), F32),
        grid=(b, dil // ncls, cls_len // tq),
        in_specs=[pl.BlockSpec((None, 1, tq, ncls * w), lambda i, r, n: (2, i, n, r)),
                  pl.BlockSpec((2, 1, CHUNK, ncls * w), lambda i, r, n: (0, i, prev_block(n), r)),
                  pl.BlockSpec((2, 1, tq, ncls * w), lambda i, r, n: (0, i, n, r)),
                  _const_spec((None, 2, ATTN_HEADS * CHUNK, 2 * CHUNK), (branch, 0, 0, 0))],
        out_specs=pl.BlockSpec((2, 1, tq, ncls * w), lambda i, r, n: (0, i, n, r)),
        compiler_params=_params(3),
        name=f"dilated_attn_{dil}",
    )(kvq, kvq, kvq, biases)


def _t5_bucket(dist):
    max_exact = REL_BUCKETS // 2
    d = jnp.maximum(dist, 1).astype(F32)
    large = max_exact + (jnp.log(d / max_exact) / math.log(REL_MAX_DIST / max_exact)
                         * (REL_BUCKETS - max_exact)).astype(jnp.int32)
    large = jnp.minimum(large, REL_BUCKETS - 1)
    return jnp.where(dist < max_exact, dist, large)


def _bias_body(rel_ref, span_ref, bucket_ref, o_ref):
    bucket = bucket_ref[0]
    row = lax.broadcasted_iota(jnp.int32, (CHUNK, 2 * CHUNK), 0)
    col = lax.broadcasted_iota(jnp.int32, (CHUNK, 2 * CHUNK), 1)
    dist = row + CHUNK - col
    band = (dist >= 0) & (dist <= span_ref[pl.program_id(0)])
    band_cur = band & (col >= CHUNK)
    for h in range(ATTN_HEADS):
        bias = jnp.zeros((CHUNK, 2 * CHUNK), F32)
        for b in range(REL_BUCKETS):
            bias = jnp.where(bucket == b, rel_ref[b, h], bias)
        bias = bias * LOG2E
        o_ref[0, 0, h * CHUNK:(h + 1) * CHUNK, :] = jnp.where(band, bias, -jnp.inf)
        o_ref[0, 1, h * CHUNK:(h + 1) * CHUNK, :] = jnp.where(band_cur, bias, -jnp.inf)


def _bias_tables(rel_bias):
    assert all(window // dil <= CHUNK for window, dil in DILATED_PATTERNS)
    dist = jnp.arange(CHUNK)[:, None] + CHUNK - jnp.arange(2 * CHUNK)[None, :]
    buckets = jnp.stack([_t5_bucket(jnp.maximum(dist, 0) * dil) for _, dil in DILATED_PATTERNS]).astype(jnp.int32)
    spans = jnp.array([window // dil for window, dil in DILATED_PATTERNS], jnp.int32)
    nb = len(DILATED_PATTERNS)
    return pl.pallas_call(
        _bias_body,
        out_shape=jax.ShapeDtypeStruct((nb, 2, ATTN_HEADS * CHUNK, 2 * CHUNK), F32),
        grid=(nb,),
        in_specs=[pl.BlockSpec(memory_space=pltpu.SMEM), pl.BlockSpec(memory_space=pltpu.SMEM),
                  pl.BlockSpec((1, CHUNK, 2 * CHUNK), lambda i: (i, 0, 0))],
        out_specs=pl.BlockSpec((1, 2, ATTN_HEADS * CHUNK, 2 * CHUNK), lambda i: (i, 0, 0, 0)),
        compiler_params=_params(1),
        name="rel_bias_tables",
    )(rel_bias, spans, buckets)


def _mix_xattn_body(h_ref, a_ref, b_ref, *rest):
    nb = len(DILATED_PATTERNS)
    branch_refs = rest[:nb]
    w_ref, gpost_ref, xpre_ref, wq_ref, kv_ref, wo_ref, xpost_ref, out_ref, nat_ref = rest[nb:]
    n_slab = ATTN_WIDTH // LANES
    off_b = SGU_WIDTH
    off_c = SGU_WIDTH + SSM_INNER
    d = h_ref.shape[-1]
    hd = d // XATTN_HEADS

    def token_order(bi, plane):
        ref, dil = branch_refs[bi], DILATED_PATTERNS[bi][1]
        if dil == 1:
            return ref[plane]
        slot = (plane * nb + bi) * n_slab
        for r in range(dil):
            for sl in range(n_slab):
                lo = r * ATTN_WIDTH + sl * LANES
                nat_ref[slot + sl, pl.ds(r, ROW_TILE // dil, stride=dil), :] = ref[plane, :, lo:lo + LANES]
        return jnp.concatenate([nat_ref[slot + sl] for sl in range(n_slab)], axis=-1)

    o1, o2, o3 = (token_order(bi, 0) for bi in range(nb))
    l1, l2, l3 = (token_order(bi, 1) for bi in range(nb))
    m = jnp.maximum(jnp.maximum(l1, l2), l3)
    e1, e2, e3 = jnp.exp2(l1 - m), jnp.exp2(l2 - m), jnp.exp2(l3 - m)
    c = (e1 * o1 + e2 * o2 + e3 * o3) * (1.0 / (e1 + e2 + e3))
    y = (_dot(a_ref[...].astype(BF16), w_ref[0:off_b, :])
         + _dot(b_ref[...].astype(BF16), w_ref[off_b:off_c, :])
         + _dot(c.astype(BF16), w_ref[off_c:off_c + ATTN_WIDTH, :]))
    h = h_ref[...] + _rms(y, gpost_ref[...])

    hn = _rms(h, xpre_ref[...]).astype(BF16)
    q = (_dot(hn, wq_ref[...]) * (hd ** -0.5 * LOG2E)).astype(BF16)
    heads = []
    for i in range(XATTN_HEADS):
        hs = slice(i * hd, (i + 1) * hd)
        logits = _dot_nt(q[:, hs], kv_ref[0, :, hs])
        m = jnp.max(logits, axis=-1, keepdims=True)
        p = jnp.exp2(logits - m)
        s = jnp.sum(p, axis=-1, keepdims=True)
        heads.append((_dot(p.astype(BF16), kv_ref[0, :, d + i * hd:d + (i + 1) * hd]) * (1.0 / s)).astype(BF16))
    y = _dot(jnp.concatenate(heads, axis=-1), wo_ref[...])
    out_ref[...] = h + _rms(y, xpost_ref[...])


def _mix_xattn(h, a, bo, branch_outs, w_out, npost, npre, wq, kv, wo, layer, tiles_per_seq):
    rows, d = h.shape
    m = kv.shape[1]
    row_spec = lambda w: pl.BlockSpec((ROW_TILE, w), lambda i: (i, 0))
    branch = [pl.BlockSpec((2, ROW_TILE // dil, ATTN_WIDTH * dil), lambda i: (0, i, 0)) for _, dil in DILATED_PATTERNS]
    return pl.pallas_call(
        _mix_xattn_body,
        out_shape=jax.ShapeDtypeStruct((rows, d), F32),
        grid=(rows // ROW_TILE,),
        in_specs=[row_spec(d), row_spec(SGU_WIDTH), row_spec(SSM_INNER), *branch,
                  _const_spec((None, d, d), (layer, 0, 0)),
                  _const_spec((None, None, 1, d), (layer, 1, 0, 0)),
                  _const_spec((None, None, 1, d), (layer, 2, 0, 0)),
                  _const_spec((None, d, d), (layer, 0, 0)),
                  pl.BlockSpec((1, m, 2 * d), lambda i: (i // tiles_per_seq, 0, 0)),
                  _const_spec((None, d, d), (layer, 0, 0)),
                  _const_spec((None, None, 1, d), (layer, 2, 0, 0))],
        out_specs=row_spec(d),
        scratch_shapes=[pltpu.VMEM((2 * len(DILATED_PATTERNS) * ATTN_WIDTH // LANES, ROW_TILE, LANES), F32)],
        compiler_params=_params(1),
        name="mix_out_xattn",
    )(h, a, bo, *branch_outs, w_out, npost, npre, wq, kv, wo, npost)


def _memkv_body(mem_ref, g_ref, w_ref, kv_ref):
    mn = _rms(mem_ref[0], g_ref[...]).astype(BF16)
    kv_ref[0] = _dot(mn, w_ref[...]).astype(BF16)


def _memkv(mem, g, wkv, layer):
    b, m, d = mem.shape
    return pl.pallas_call(
        _memkv_body,
        out_shape=jax.ShapeDtypeStruct((b, m, 2 * d), BF16),
        grid=(b,),
        in_specs=[pl.BlockSpec((1, m, d), lambda i: (i, 0, 0)), _const_spec((None, 1, d), (layer, 0, 0)),
                  _const_spec((None, d, 2 * d), (layer, 0, 0))],
        out_specs=pl.BlockSpec((1, m, 2 * d), lambda i: (i, 0, 0)),
        compiler_params=_params(1),
        name="mem_kv",
    )(mem, g, wkv)


def kernel(x, mem, norm_pre, norm_post, ffn_wi, ffn_wo, mix_w_in, mix_w_out, sgu_ln_g, sgu_w, sgu_b, ssm_conv_w, ssm_conv_b, ssm_dt_bias, ssm_a_log, ssm_d, ssm_norm_g, rel_bias, mem_norm_g, xattn_wq, xattn_wkv, xattn_wo):
    b, s, d = x.shape
    depth = norm_pre.shape[0]
    assert s % ATTN_TILE == 0 and s % ROW_TILE == 0 and s % SSD_TILE == 0
    assert all(s % (dil * CHUNK) == 0 for _, dil in DILATED_PATTERNS)

    npre = norm_pre[:, :, None, :]
    npost = norm_post[:, :, None, :]
    wi = ffn_wi.astype(BF16)
    wo = ffn_wo.astype(BF16)
    w_main = jnp.concatenate([mix_w_in[..., :OFF_DT], mix_w_in[..., OFF_QKV:]], axis=-1).astype(BF16)
    w_dt = jnp.repeat(mix_w_in[..., OFF_DT:OFF_QKV], HEAD_DIM, axis=-1).astype(BF16)
    w_out = mix_w_out.astype(BF16)
    lng = sgu_ln_g[:, None, :]
    sgu_b_e = jnp.repeat(jnp.swapaxes(sgu_b, 1, 2), HEAD_DIM, axis=-1)
    per_head = lambda p: jnp.repeat(p, HEAD_DIM, axis=-1)[:, None, :]
    dtb, alog, dskip = per_head(ssm_dt_bias), per_head(ssm_a_log), per_head(ssm_d)
    conv_b = ssm_conv_b[:, None, :]
    norm_g = ssm_norm_g[:, None, :]
    mem_g = mem_norm_g[:, None, :]
    wq, wkv, wxo = xattn_wq.astype(BF16), xattn_wkv.astype(BF16), xattn_wo.astype(BF16)
    biases = _bias_tables(rel_bias)

    h = x.reshape(b * s, d)
    for l in range(depth):
        h = _ffn(h, npre, wi, wo, npost, l, 0)
        a, zxd, bc, *kvq = _inproj(h, npre, w_main, w_dt, lng, sgu_w, sgu_b_e, ssm_conv_w, conv_b, dtb, l,
                                   s // ROW_TILE)
        seq = lambda t: t.reshape(b, s, t.shape[-1])
        bo = _ssd(seq(zxd), seq(bc), alog, dskip, norm_g, l)
        branch_outs = []
        for branch, t in enumerate(kvq):
            ol = _attn_branch(t.reshape(3, b, -1, t.shape[-1]), biases, branch)
            branch_outs.append(ol.reshape(2, -1, ol.shape[-1]))
        mkv = _memkv(mem, mem_g, wkv, l)
        h = _mix_xattn(h, a, bo.reshape(b * s, SSM_INNER), branch_outs, w_out, npost, npre, wq, mkv, wxo, l,
                       s // ROW_TILE)
        h = _ffn(h, npre, wi, wo, npost, l, 1)
    return h.reshape(b, s, d)
```

```python
import functools
import math

import jax
import jax.numpy as jnp
from jax import lax
from jax.experimental import pallas as pl
from jax.experimental.pallas import tpu as pltpu

F32 = jnp.float32
BF16 = jnp.bfloat16
EPS = 1e-6
LOG2E = math.log2(math.e)

LANES = 128
MXU_N = 256
HEAD_DIM = 64
CHUNK = 128
SGU_WIDTH = 256
SGU_GROUPS = SGU_WIDTH // HEAD_DIM
SSM_INNER = 512
SSM_HEADS = SSM_INNER // HEAD_DIM
SSM_GROUPS = 2
SSM_STATE = 128
SSM_CONV = 4
SSM_CONV_CH = SSM_INNER + 2 * SSM_GROUPS * SSM_STATE
GROUP_W = SSM_INNER // SSM_GROUPS
ATTN_WIDTH = 256
ATTN_HEADS = ATTN_WIDTH // HEAD_DIM
DILATED_PATTERNS = ((128, 1), (512, 4), (2048, 16))
REL_BUCKETS = 32
REL_MAX_DIST = 2048
XATTN_HEADS = 4

OFF_Z = 2 * SGU_WIDTH
OFF_XBC = OFF_Z + SSM_INNER
OFF_DT = OFF_XBC + SSM_CONV_CH
OFF_QKV = OFF_DT + SSM_HEADS
W_MAIN = OFF_DT + 3 * ATTN_WIDTH

ROW_TILE = 512
FFN_ROW_TILE = 1024
FF_TILE = 256
SSD_TILE = 1024
ATTN_TILE = 2048
CONV_PAD = 8
VMEM_LIMIT = 56 * 1024 * 1024
SINGLE_BUFFER_MIN_ELEMS = 1 << 19


def _params(n_axes):
    return pltpu.CompilerParams(dimension_semantics=("arbitrary",) * n_axes,
                                vmem_limit_bytes=VMEM_LIMIT)


def _rms(x, g):
    return x * lax.rsqrt(jnp.mean(x * x, axis=-1, keepdims=True) + EPS) * g


def _silu(x):
    return x * jax.nn.sigmoid(x)


def _dot(a, b):
    return jnp.dot(a, b, preferred_element_type=F32)


def _dot_nt(a, b):
    return lax.dot_general(a, b, (((1,), (1,)), ((), ())), preferred_element_type=F32)


def _const_spec(shape, index):
    n_elems = math.prod(dim for dim in shape if dim is not None)
    mode = pl.Buffered(1) if n_elems >= SINGLE_BUFFER_MIN_ELEMS else None
    return pl.BlockSpec(shape, lambda *_: index, pipeline_mode=mode)


def _ffn_body(h_ref, gpre_ref, wi_ref, wo_ref, gpost_ref, o_ref, acc_ref, *, d_ff):
    for r0 in range(0, FFN_ROW_TILE, ROW_TILE):
        rows = slice(r0, r0 + ROW_TILE)
        h = h_ref[rows, :]
        hn = _rms(h, gpre_ref[...]).astype(BF16)
        for c in range(d_ff // FF_TILE):
            lo = c * FF_TILE
            g = _dot(hn, wi_ref[:, lo:lo + FF_TILE])
            u = _dot(hn, wi_ref[:, d_ff + lo:d_ff + lo + FF_TILE])
            y = _dot((_silu(g) * u).astype(BF16), wo_ref[lo:lo + FF_TILE, :])
            if c == 0:
                acc_ref[rows, :] = y
            else:
                acc_ref[rows, :] += y
        o_ref[rows, :] = h + 0.5 * _rms(acc_ref[rows, :], gpost_ref[...])


def _ffn(h, gpre, wi, wo, gpost, layer, which):
    rows, d = h.shape
    d_ff = wo.shape[2]
    return pl.pallas_call(
        functools.partial(_ffn_body, d_ff=d_ff),
        out_shape=jax.ShapeDtypeStruct((rows, d), F32),
        grid=(rows // FFN_ROW_TILE,),
        in_specs=[
            pl.BlockSpec((FFN_ROW_TILE, d), lambda i: (i, 0)),
            _const_spec((None, None, 1, d), (layer, 3 * which, 0, 0)),
            _const_spec((None, None, d, 2 * d_ff), (layer, which, 0, 0)),
            _const_spec((None, None, d_ff, d), (layer, which, 0, 0)),
            _const_spec((None, None, 1, d), (layer, 3 * which, 0, 0)),
        ],
        out_specs=pl.BlockSpec((FFN_ROW_TILE, d), lambda i: (i, 0)),
        scratch_shapes=[pltpu.VMEM((FFN_ROW_TILE, d), F32)],
        compiler_params=_params(1),
        name="ffn",
    )(h, gpre, wi, wo, gpost)


def _inproj_body(h_ref, gpre_ref, w_ref, wdt_ref, lng_ref, sw_ref, sb_ref, cw_ref, cb_ref, dtb_ref,
                 a_ref, zxd_ref, bc_ref, *rest, tiles_per_seq):
    kvq_refs, (slab_ref, ext_ref) = rest[:-2], rest[-2:]

    first = pl.program_id(0) % tiles_per_seq == 0

    @pl.when(first)
    def _():
        ext_ref[0:CONV_PAD, :] = jnp.zeros((CONV_PAD, SSM_CONV_CH), F32)

    @pl.when(jnp.logical_not(first))
    def _():
        ext_ref[0:CONV_PAD, :] = ext_ref[ROW_TILE:ROW_TILE + CONV_PAD, :]

    hn = _rms(h_ref[...], gpre_ref[...]).astype(BF16)

    def proj(lo, hi):
        return _dot(hn, w_ref[:, lo:hi])

    def stage_xbc(g):
        lo = g * MXU_N
        ext_ref[CONV_PAD:CONV_PAD + ROW_TILE, lo:lo + MXU_N] = proj(OFF_XBC + lo, OFF_XBC + lo + MXU_N)

    def conv_chunk(c, lo, hi, out_ref, out_lo):
        r0 = c * CHUNK
        for l0 in range(lo, hi, LANES):
            ls = slice(l0, l0 + LANES)
            win = ext_ref[r0:r0 + CONV_PAD + CHUNK, ls]
            acc = cb_ref[:, ls] + win[CONV_PAD:] * cw_ref[SSM_CONV - 1:SSM_CONV, ls]
            for back in range(1, SSM_CONV):
                tap = SSM_CONV - 1 - back
                acc = acc + pltpu.roll(win, back, 0)[CONV_PAD:] * cw_ref[tap:tap + 1, ls]
            o0 = out_lo + l0 - lo
            out_ref[r0:r0 + CHUNK, o0:o0 + LANES] = _silu(acc).astype(out_ref.dtype)

    conv_x = lambda c: conv_chunk(c, 0, SSM_INNER, zxd_ref, SSM_INNER)
    conv_bc = lambda c: conv_chunk(c, SSM_INNER, SSM_CONV_CH, bc_ref, 0)

    def dt_half(g):
        ls = slice(g * MXU_N, (g + 1) * MXU_N)
        zxd_ref[:, 2 * SSM_INNER + g * MXU_N:2 * SSM_INNER + (g + 1) * MXU_N] = (
            _softplus(_dot(hn, wdt_ref[:, ls]) + dtb_ref[:, ls]))

    uv = proj(0, OFF_Z)
    stage_xbc(0)
    stage_xbc(1)

    uv = jax.nn.gelu(uv)
    u = uv[:, :SGU_WIDTH]
    v = uv[:, SGU_WIDTH:]
    mu = jnp.mean(v, axis=-1, keepdims=True)
    var = jnp.mean(jnp.square(v - mu), axis=-1, keepdims=True)
    vn = ((v - mu) * lax.rsqrt(var + EPS) * lng_ref[...]).astype(BF16)
    causal = (lax.broadcasted_iota(jnp.int32, (CHUNK, CHUNK), 0)
              >= lax.broadcasted_iota(jnp.int32, (CHUNK, CHUNK), 1))
    lane_group = lax.broadcasted_iota(jnp.int32, (1, SGU_WIDTH), 1) // HEAD_DIM
    w_cat = jnp.concatenate([jnp.where(causal, sw_ref[g], 0.0).astype(BF16) for g in range(SGU_GROUPS)], axis=1)
    for c in range(ROW_TILE // CHUNK):
        r0 = c * CHUNK
        vc = vn[r0:r0 + CHUNK]
        v_blocks = [jnp.where(lane_group == g, vc, jnp.zeros_like(vc)) for g in range(SGU_GROUPS)]
        s = sb_ref[...] + _dot(w_cat, jnp.concatenate(v_blocks, axis=0))
        a_ref[r0:r0 + CHUNK, :] = (u[r0:r0 + CHUNK] * s).astype(BF16)

    stage_xbc(2)
    conv_x(0)
    stage_xbc(3)
    conv_x(1)

    n_slab = ATTN_WIDTH // LANES
    for t, vector_step in enumerate((lambda: conv_x(2), lambda: conv_x(3), lambda: conv_bc(0))):
        val = proj(OFF_DT + t * ATTN_WIDTH, OFF_DT + (t + 1) * ATTN_WIDTH)
        if t == 0:
            val = val * (HEAD_DIM ** -0.5 * LOG2E)
        for sl in range(n_slab):
            slab_ref[t * n_slab + sl] = val[:, sl * LANES:(sl + 1) * LANES]
        plane = (t + 2) % 3
        for bi, (_, dil) in enumerate(DILATED_PATTERNS):
            out = kvq_refs[bi]
            if dil == 1:
                out[plane] = val.astype(BF16)
                continue
            for r in range(dil):
                for sl in range(n_slab):
                    lo = r * ATTN_WIDTH + sl * LANES
                    out[plane, :, lo:lo + LANES] = (
                        slab_ref[t * n_slab + sl, pl.ds(r, ROW_TILE // dil, stride=dil), :].astype(BF16))
        vector_step()

    dt_half(0)
    conv_bc(1)
    dt_half(1)
    conv_bc(2)
    zxd_ref[:, 0:MXU_N] = proj(OFF_Z, OFF_Z + MXU_N)
    conv_bc(3)
    zxd_ref[:, MXU_N:SSM_INNER] = proj(OFF_Z + MXU_N, OFF_XBC)


def _inproj(h, gpre, w_main, w_dt, lng, sgu_w, sgu_b, conv_w, conv_b, dtb, layer, tiles_per_seq):
    rows, d = h.shape
    row_spec = lambda w: pl.BlockSpec((ROW_TILE, w), lambda i: (i, 0))
    vec = lambda w: _const_spec((None, 1, w), (layer, 0, 0))
    outs = ((SGU_WIDTH, BF16), (3 * SSM_INNER, F32), (SSM_CONV_CH - SSM_INNER, BF16))
    out_shape = [jax.ShapeDtypeStruct((rows, w), t) for w, t in outs]
    out_specs = [row_spec(w) for w, _ in outs]
    for _, dil in DILATED_PATTERNS:
        out_shape.append(jax.ShapeDtypeStruct((3, rows // dil, ATTN_WIDTH * dil), BF16))
        out_specs.append(pl.BlockSpec((3, ROW_TILE // dil, ATTN_WIDTH * dil), lambda i: (0, i, 0)))
    return pl.pallas_call(
        functools.partial(_inproj_body, tiles_per_seq=tiles_per_seq),
        out_shape=out_shape,
        grid=(rows // ROW_TILE,),
        in_specs=[
            row_spec(d),
            _const_spec((None, None, 1, d), (layer, 1, 0, 0)),
            _const_spec((None, d, W_MAIN), (layer, 0, 0)),
            _const_spec((None, d, SSM_INNER), (layer, 0, 0)),
            vec(SGU_WIDTH),
            _const_spec((None, SGU_GROUPS, CHUNK, CHUNK), (layer, 0, 0, 0)),
            _const_spec((None, CHUNK, SGU_WIDTH), (layer, 0, 0)),
            _const_spec((None, SSM_CONV, SSM_CONV_CH), (layer, 0, 0)),
            vec(SSM_CONV_CH), vec(SSM_INNER),
        ],
        out_specs=out_specs,
        scratch_shapes=[pltpu.VMEM((3 * ATTN_WIDTH // LANES, ROW_TILE, LANES), F32),
                        pltpu.VMEM((CONV_PAD + ROW_TILE, SSM_CONV_CH), F32)],
        compiler_params=_params(1),
        name="inproj_sgu",
    )(h, gpre, w_main, w_dt, lng, sgu_w, sgu_b, conv_w, conv_b, dtb)


def _softplus(x):
    return jnp.maximum(x, 0.0) + jnp.log1p(jnp.exp(-jnp.abs(x)))


def _cumsum_rows(tril, a):
    hi = a.astype(BF16)
    r1 = a - hi.astype(F32)
    mid = r1.astype(BF16)
    lo = (r1 - mid.astype(F32)).astype(BF16)
    return _dot(tril, hi) + _dot(tril, mid) + _dot(tril, lo)


def _ssd_body(zxd_ref, bc_ref, alog_ref, d_ref, ng_ref, o_ref, hs_ref):
    @pl.when(pl.program_id(1) == 0)
    def _():
        hs_ref[...] = jnp.zeros_like(hs_ref)

    row = lax.broadcasted_iota(jnp.int32, (CHUNK, CHUNK), 0)
    col = lax.broadcasted_iota(jnp.int32, (CHUNK, CHUNK), 1)
    causal = row >= col
    tril = jnp.where(causal, 1.0, 0.0).astype(BF16)
    low_half = col < HEAD_DIM
    lane_head = lax.broadcasted_iota(jnp.int32, (1, GROUP_W), 1) // HEAD_DIM
    a_neg = -jnp.exp(alog_ref[...]) * LOG2E
    c_off = SSM_GROUPS * SSM_STATE

    for c in range(SSD_TILE // CHUNK):
        r0 = c * CHUNK
        xs = zxd_ref[0, r0:r0 + CHUNK, SSM_INNER:2 * SSM_INNER]
        dt = zxd_ref[0, r0:r0 + CHUNK, 2 * SSM_INNER:3 * SSM_INNER]
        acs = _cumsum_rows(tril, dt * a_neg)
        acs_last = acs[CHUNK - 1:CHUNK, :]
        eacs = jnp.exp2(acs)
        x = xs * dt
        xb = x.astype(BF16)
        xd = (x * jnp.exp2(acs_last - acs)).astype(BF16)
        for g in range(SSM_GROUPS):
            gl = slice(g * GROUP_W, (g + 1) * GROUP_W)
            bg = bc_ref[0, r0:r0 + CHUNK, g * SSM_STATE:(g + 1) * SSM_STATE]
            cg = bc_ref[0, r0:r0 + CHUNK, c_off + g * SSM_STATE:c_off + (g + 1) * SSM_STATE]
            cbm = _dot_nt(cg, bg)
            xg = xb[:, gl]
            weights = []
            for e in range(SSM_HEADS // SSM_GROUPS):
                head = g * (SSM_HEADS // SSM_GROUPS) + e
                pair = acs[:, (head // 2) * 2 * HEAD_DIM:(head // 2 + 1) * 2 * HEAD_DIM]
                swapped = pltpu.roll(pair, HEAD_DIM, 1)
                acs_col = jnp.where(low_half, pair, swapped) if head % 2 == 0 else jnp.where(low_half, swapped, pair)
                seg = acs_col - acs_col.T
                decay = jnp.exp2(jnp.where(causal, seg, -jnp.inf))
                weights.append((cbm * decay).astype(BF16))
            x_blocks = [jnp.where(lane_head == e, xg, jnp.zeros_like(xg)) for e in range(SSM_HEADS // SSM_GROUPS)]
            yg = _dot(jnp.concatenate(weights, axis=1), jnp.concatenate(x_blocks, axis=0))
            h_prev = hs_ref[g]
            y_off = _dot(cg, h_prev.astype(BF16)) * eacs[:, gl]
            hs_ref[g] = h_prev * eacs[CHUNK - 1:CHUNK, gl] + _dot(bg.astype(F32).T.astype(BF16), xd[:, gl])
            y = yg + y_off + d_ref[:, gl] * xs[:, gl]
            y = y * _silu(zxd_ref[0, r0:r0 + CHUNK, gl])
            o_ref[0, r0:r0 + CHUNK, gl] = _rms(y, ng_ref[:, gl]).astype(BF16)


def _ssd(zxd, bc, alog, dskip, norm_g, layer):
    b, s, _ = zxd.shape
    tile = lambda w: pl.BlockSpec((1, SSD_TILE, w), lambda i, j: (i, j, 0))
    vec = lambda w: _const_spec((None, 1, w), (layer, 0, 0))
    return pl.pallas_call(
        _ssd_body,
        out_shape=jax.ShapeDtypeStruct((b, s, SSM_INNER), BF16),
        grid=(b, s // SSD_TILE),
        in_specs=[tile(3 * SSM_INNER), tile(SSM_CONV_CH - SSM_INNER), vec(SSM_INNER), vec(SSM_INNER), vec(SSM_INNER)],
        out_specs=tile(SSM_INNER),
        scratch_shapes=[pltpu.VMEM((SSM_GROUPS, SSM_STATE, GROUP_W), F32)],
        compiler_params=_params(2),
        name="ssd",
    )(zxd, bc, alog, dskip, norm_g)


def _attn_body(q_ref, kvp_ref, kvc_ref, bias_ref, ol_ref, *, tq, ncls):
    n = pl.program_id(2)
    nblk = tq // CHUNK
    w = ATTN_WIDTH
    lane_head = lax.broadcasted_iota(jnp.int32, (1, w), 1) // HEAD_DIM
    first_table = jnp.where(n == 0, 1, 0)
    for c in range(ncls):
        cl = slice(c * w, (c + 1) * w)
        for j in range(nblk):
            rows = slice(j * CHUNK, (j + 1) * CHUNK)
            q = q_ref[0, rows, cl]
            lhs = jnp.concatenate([jnp.where(lane_head == h, q, jnp.zeros_like(q)) for h in range(ATTN_HEADS)], axis=0)
            if j == 0:
                kk, vv = (jnp.concatenate([kvp_ref[t, 0, :, cl], kvc_ref[t, 0, 0:CHUNK, cl]], axis=0) for t in (0, 1))
            else:
                kk, vv = (kvc_ref[t, 0, (j - 1) * CHUNK:(j + 1) * CHUNK, cl] for t in (0, 1))
            logits = _dot_nt(lhs, kk) + bias_ref[first_table if j == 0 else 0]
            m = jnp.max(logits, axis=-1, keepdims=True)
            p = jnp.exp2(logits - m)
            s = jnp.sum(p, axis=-1, keepdims=True)
            pv = _dot(p.astype(BF16), vv) * (1.0 / s)
            lse = m + jnp.log2(s)
            o_acc = pv[0:CHUNK]
            lse_acc = jnp.broadcast_to(lse[0:CHUNK], (CHUNK, w))
            for h in range(1, ATTN_HEADS):
                hr = slice(h * CHUNK, (h + 1) * CHUNK)
                o_acc = jnp.where(lane_head == h, pv[hr], o_acc)
                lse_acc = jnp.where(lane_head == h, lse[hr], lse_acc)
            ol_ref[0, 0, rows, cl] = o_acc
            ol_ref[1, 0, rows, cl] = lse_acc


def _attn_branch(kvq, biases, branch):
    dil = DILATED_PATTERNS[branch][1]
    w = ATTN_WIDTH
    _, b, cls_len, _ = kvq.shape
    tq = min(ATTN_TILE, cls_len)
    ncls = min(dil, ATTN_TILE // tq)
    prev_block = lambda n: jnp.maximum(n * (tq // CHUNK) - 1, 0)
    return pl.pallas_call(
        functools.partial(_attn_body, tq=tq, ncls=ncls),
        out_shape=jax.ShapeDtypeStruct((2, b, cls_len, dil * w), F32),
        grid=(b, dil // ncls, cls_len // tq),
        in_specs=[pl.BlockSpec((None, 1, tq, ncls * w), lambda i, r, n: (2, i, n, r)),
                  pl.BlockSpec((2, 1, CHUNK, ncls * w), lambda i, r, n: (0, i, prev_block(n), r)),
                  pl.BlockSpec((2, 1, tq, ncls * w), lambda i, r, n: (0, i, n, r)),
                  _const_spec((None, 2, ATTN_HEADS * CHUNK, 2 * CHUNK), (branch, 0, 0, 0))],
        out_specs=pl.BlockSpec((2, 1, tq, ncls * w), lambda i, r, n: (0, i, n, r)),
        compiler_params=_params(3),
        name=f"dilated_attn_{dil}",
    )(kvq, kvq, kvq, biases)


def _t5_bucket(dist):
    max_exact = REL_BUCKETS // 2
    d = jnp.maximum(dist, 1).astype(F32)
    large = max_exact + (jnp.log(d / max_exact) / math.log(REL_MAX_DIST / max_exact)
                         * (REL_BUCKETS - max_exact)).astype(jnp.int32)
    large = jnp.minimum(large, REL_BUCKETS - 1)
    return jnp.where(dist < max_exact, dist, large)


def _bias_body(rel_ref, span_ref, bucket_ref, o_ref):
    bucket = bucket_ref[0]
    row = lax.broadcasted_iota(jnp.int32, (CHUNK, 2 * CHUNK), 0)
    col = lax.broadcasted_iota(jnp.int32, (CHUNK, 2 * CHUNK), 1)
    dist = row + CHUNK - col
    band = (dist >= 0) & (dist <= span_ref[pl.program_id(0)])
    band_cur = band & (col >= CHUNK)
    for h in range(ATTN_HEADS):
        bias = jnp.zeros((CHUNK, 2 * CHUNK), F32)
        for b in range(REL_BUCKETS):
            bias = jnp.where(bucket == b, rel_ref[b, h], bias)
        bias = bias * LOG2E
        o_ref[0, 0, h * CHUNK:(h + 1) * CHUNK, :] = jnp.where(band, bias, -jnp.inf)
        o_ref[0, 1, h * CHUNK:(h + 1) * CHUNK, :] = jnp.where(band_cur, bias, -jnp.inf)


def _bias_tables(rel_bias):
    assert all(window // dil <= CHUNK for window, dil in DILATED_PATTERNS)
    dist = jnp.arange(CHUNK)[:, None] + CHUNK - jnp.arange(2 * CHUNK)[None, :]
    buckets = jnp.stack([_t5_bucket(jnp.maximum(dist, 0) * dil) for _, dil in DILATED_PATTERNS]).astype(jnp.int32)
    spans = jnp.array([window // dil for window, dil in DILATED_PATTERNS], jnp.int32)
    nb = len(DILATED_PATTERNS)
    return pl.pallas_call(
        _bias_body,
        out_shape=jax.ShapeDtypeStruct((nb, 2, ATTN_HEADS * CHUNK, 2 * CHUNK), F32),
        grid=(nb,),
        in_specs=[pl.BlockSpec(memory_space=pltpu.SMEM), pl.BlockSpec(memory_space=pltpu.SMEM),
                  pl.BlockSpec((1, CHUNK, 2 * CHUNK), lambda i: (i, 0, 0))],
        out_specs=pl.BlockSpec((1, 2, ATTN_HEADS * CHUNK, 2 * CHUNK), lambda i: (i, 0, 0, 0)),
        compiler_params=_params(1),
        name="rel_bias_tables",
    )(rel_bias, spans, buckets)


def _mix_xattn_body(h_ref, a_ref, b_ref, *rest):
    nb = len(DILATED_PATTERNS)
    branch_refs = rest[:nb]
    w_ref, gpost_ref, xpre_ref, wq_ref, kv_ref, wo_ref, xpost_ref, out_ref, nat_ref = rest[nb:]
    n_slab = ATTN_WIDTH // LANES
    off_b = SGU_WIDTH
    off_c = SGU_WIDTH + SSM_INNER
    d = h_ref.shape[-1]
    hd = d // XATTN_HEADS

    def token_order(bi, plane):
        ref, dil = branch_refs[bi], DILATED_PATTERNS[bi][1]
        if dil == 1:
            return ref[plane]
        slot = (plane * nb + bi) * n_slab
        for r in range(dil):
            for sl in range(n_slab):
                lo = r * ATTN_WIDTH + sl * LANES
                nat_ref[slot + sl, pl.ds(r, ROW_TILE // dil, stride=dil), :] = ref[plane, :, lo:lo + LANES]
        return jnp.concatenate([nat_ref[slot + sl] for sl in range(n_slab)], axis=-1)

    o1, o2, o3 = (token_order(bi, 0) for bi in range(nb))
    l1, l2, l3 = (token_order(bi, 1) for bi in range(nb))
    m = jnp.maximum(jnp.maximum(l1, l2), l3)
    e1, e2, e3 = jnp.exp2(l1 - m), jnp.exp2(l2 - m), jnp.exp2(l3 - m)
    c = (e1 * o1 + e2 * o2 + e3 * o3) * (1.0 / (e1 + e2 + e3))
    y = (_dot(a_ref[...], w_ref[0:off_b, :])
         + _dot(b_ref[...], w_ref[off_b:off_c, :])
         + _dot(c.astype(BF16), w_ref[off_c:off_c + ATTN_WIDTH, :]))
    h = h_ref[...] + _rms(y, gpost_ref[...])

    hn = _rms(h, xpre_ref[...]).astype(BF16)
    q = (_dot(hn, wq_ref[...]) * (hd ** -0.5 * LOG2E)).astype(BF16)
    heads = []
    for i in range(XATTN_HEADS):
        hs = slice(i * hd, (i + 1) * hd)
        logits = _dot_nt(q[:, hs], kv_ref[0, :, hs])
        m = jnp.max(logits, axis=-1, keepdims=True)
        p = jnp.exp2(logits - m)
        s = jnp.sum(p, axis=-1, keepdims=True)
        heads.append((_dot(p.astype(BF16), kv_ref[0, :, d + i * hd:d + (i + 1) * hd]) * (1.0 / s)).astype(BF16))
    y = _dot(jnp.concatenate(heads, axis=-1), wo_ref[...])
    out_ref[...] = h + _rms(y, xpost_ref[...])


def _mix_xattn(h, a, bo, branch_outs, w_out, npost, npre, wq, kv, wo, layer, tiles_per_seq):
    rows, d = h.shape
    m = kv.shape[2]
    row_spec = lambda w: pl.BlockSpec((ROW_TILE, w), lambda i: (i, 0))
    branch = [pl.BlockSpec((2, ROW_TILE // dil, ATTN_WIDTH * dil), lambda i: (0, i, 0)) for _, dil in DILATED_PATTERNS]
    return pl.pallas_call(
        _mix_xattn_body,
        out_shape=jax.ShapeDtypeStruct((rows, d), F32),
        grid=(rows // ROW_TILE,),
        in_specs=[row_spec(d), row_spec(SGU_WIDTH), row_spec(SSM_INNER), *branch,
                  _const_spec((None, d, d), (layer, 0, 0)),
                  _const_spec((None, None, 1, d), (layer, 1, 0, 0)),
                  _const_spec((None, None, 1, d), (layer, 2, 0, 0)),
                  _const_spec((None, d, d), (layer, 0, 0)),
                  pl.BlockSpec((None, 1, m, 2 * d), lambda i: (layer, i // tiles_per_seq, 0, 0)),
                  _const_spec((None, d, d), (layer, 0, 0)),
                  _const_spec((None, None, 1, d), (layer, 2, 0, 0))],
        out_specs=row_spec(d),
        scratch_shapes=[pltpu.VMEM((2 * len(DILATED_PATTERNS) * ATTN_WIDTH // LANES, ROW_TILE, LANES), F32)],
        compiler_params=_params(1),
        name="mix_out_xattn",
    )(h, a, bo, *branch_outs, w_out, npost, npre, wq, kv, wo, npost)


def _memkv_body(mem_ref, g_ref, w_ref, kv_ref):
    mn = _rms(mem_ref[0], g_ref[...]).astype(BF16)
    kv_ref[0] = _dot(mn, w_ref[...]).astype(BF16)


def _memkv(mem, g, wkv):
    b, m, d = mem.shape
    depth = wkv.shape[0]
    return pl.pallas_call(
        _memkv_body,
        out_shape=jax.ShapeDtypeStruct((depth, b, m, 2 * d), BF16),
        grid=(depth, b),
        in_specs=[pl.BlockSpec((1, m, d), lambda l, i: (i, 0, 0)),
                  pl.BlockSpec((None, 1, d), lambda l, i: (l, 0, 0)),
                  pl.BlockSpec((None, d, 2 * d), lambda l, i: (l, 0, 0))],
        out_specs=pl.BlockSpec((None, 1, m, 2 * d), lambda l, i: (l, i, 0, 0)),
        compiler_params=_params(2),
        name="mem_kv",
    )(mem, g, wkv)


def kernel(x, mem, norm_pre, norm_post, ffn_wi, ffn_wo, mix_w_in, mix_w_out, sgu_ln_g, sgu_w, sgu_b, ssm_conv_w, ssm_conv_b, ssm_dt_bias, ssm_a_log, ssm_d, ssm_norm_g, rel_bias, mem_norm_g, xattn_wq, xattn_wkv, xattn_wo):
    b, s, d = x.shape
    depth = norm_pre.shape[0]
    assert s % ATTN_TILE == 0 and s % ROW_TILE == 0 and s % SSD_TILE == 0
    assert all(s % (dil * CHUNK) == 0 for _, dil in DILATED_PATTERNS)

    npre = norm_pre[:, :, None, :]
    npost = norm_post[:, :, None, :]
    wi = ffn_wi.astype(BF16)
    wo = ffn_wo.astype(BF16)
    w_main = jnp.concatenate([mix_w_in[..., :OFF_DT], mix_w_in[..., OFF_QKV:]], axis=-1).astype(BF16)
    w_dt = jnp.repeat(mix_w_in[..., OFF_DT:OFF_QKV], HEAD_DIM, axis=-1).astype(BF16)
    w_out = mix_w_out.astype(BF16)
    lng = sgu_ln_g[:, None, :]
    sgu_b_e = jnp.repeat(jnp.swapaxes(sgu_b, 1, 2), HEAD_DIM, axis=-1)
    per_head = lambda p: jnp.repeat(p, HEAD_DIM, axis=-1)[:, None, :]
    dtb, alog, dskip = per_head(ssm_dt_bias), per_head(ssm_a_log), per_head(ssm_d)
    conv_b = ssm_conv_b[:, None, :]
    norm_g = ssm_norm_g[:, None, :]
    mem_g = mem_norm_g[:, None, :]
    wq, wkv, wxo = xattn_wq.astype(BF16), xattn_wkv.astype(BF16), xattn_wo.astype(BF16)
    biases = _bias_tables(rel_bias)
    mkv = _memkv(mem, mem_g, wkv)

    h = x.reshape(b * s, d)
    for l in range(depth):
        h = _ffn(h, npre, wi, wo, npost, l, 0)
        a, zxd, bc, *kvq = _inproj(h, npre, w_main, w_dt, lng, sgu_w, sgu_b_e, ssm_conv_w, conv_b, dtb, l,
                                   s // ROW_TILE)
        seq = lambda t: t.reshape(b, s, t.shape[-1])
        bo = _ssd(seq(zxd), seq(bc), alog, dskip, norm_g, l)
        branch_outs = []
        for branch, t in enumerate(kvq):
            ol = _attn_branch(t.reshape(3, b, -1, t.shape[-1]), biases, branch)
            branch_outs.append(ol.reshape(2, -1, ol.shape[-1]))
        h = _mix_xattn(h, a, bo.reshape(b * s, SSM_INNER), branch_outs, w_out, npost, npre, wq, mkv, wxo, l,
                       s // ROW_TILE)
        h = _ffn(h, npre, wi, wo, npost, l, 1)
    return h.reshape(b, s, d)
```

```python
import functools
import math

import jax
import jax.numpy as jnp
from jax import lax
from jax.experimental import pallas as pl
from jax.experimental.pallas import tpu as pltpu

F32 = jnp.float32
BF16 = jnp.bfloat16
EPS = 1e-6
LOG2E = math.log2(math.e)

LANES = 128
MXU_N = 256
HEAD_DIM = 64
CHUNK = 128
SGU_WIDTH = 256
SGU_GROUPS = SGU_WIDTH // HEAD_DIM
SSM_INNER = 512
SSM_HEADS = SSM_INNER // HEAD_DIM
SSM_GROUPS = 2
SSM_STATE = 128
SSM_CONV = 4
SSM_CONV_CH = SSM_INNER + 2 * SSM_GROUPS * SSM_STATE
GROUP_W = SSM_INNER // SSM_GROUPS
ATTN_WIDTH = 256
ATTN_HEADS = ATTN_WIDTH // HEAD_DIM
DILATED_PATTERNS = ((128, 1), (512, 4), (2048, 16))
REL_BUCKETS = 32
REL_MAX_DIST = 2048
XATTN_HEADS = 4

OFF_Z = 2 * SGU_WIDTH
OFF_XBC = OFF_Z + SSM_INNER
OFF_DT = OFF_XBC + SSM_CONV_CH
OFF_QKV = OFF_DT + SSM_HEADS
W_MAIN = OFF_DT + 3 * ATTN_WIDTH

ROW_TILE = 512
FFN_ROW_TILE = 1024
FF_TILE = 256
SSD_TILE = 1024
ATTN_TILE = 4096
CONV_PAD = 8
VMEM_LIMIT = 56 * 1024 * 1024
SINGLE_BUFFER_MIN_ELEMS = 1 << 19


def _params(n_axes):
    return pltpu.CompilerParams(dimension_semantics=("arbitrary",) * n_axes,
                                vmem_limit_bytes=VMEM_LIMIT)


def _rms(x, g):
    return x * lax.rsqrt(jnp.mean(x * x, axis=-1, keepdims=True) + EPS) * g


def _silu(x):
    return x * jax.nn.sigmoid(x)


def _dot(a, b):
    return jnp.dot(a, b, preferred_element_type=F32)


def _dot_nt(a, b):
    return lax.dot_general(a, b, (((1,), (1,)), ((), ())), preferred_element_type=F32)


def _const_spec(shape, index):
    n_elems = math.prod(dim for dim in shape if dim is not None)
    mode = pl.Buffered(1) if n_elems >= SINGLE_BUFFER_MIN_ELEMS else None
    return pl.BlockSpec(shape, lambda *_: index, pipeline_mode=mode)


def _ffn_body(h_ref, gpre_ref, wi_ref, wo_ref, gpost_ref, o_ref, acc_ref, *, d_ff):
    for r0 in range(0, FFN_ROW_TILE, ROW_TILE):
        rows = slice(r0, r0 + ROW_TILE)
        h = h_ref[rows, :]
        hn = _rms(h, gpre_ref[...]).astype(BF16)
        for c in range(d_ff // FF_TILE):
            lo = c * FF_TILE
            g = _dot(hn, wi_ref[:, lo:lo + FF_TILE])
            u = _dot(hn, wi_ref[:, d_ff + lo:d_ff + lo + FF_TILE])
            y = _dot((_silu(g) * u).astype(BF16), wo_ref[lo:lo + FF_TILE, :])
            if c == 0:
                acc_ref[rows, :] = y
            else:
                acc_ref[rows, :] += y
        o_ref[rows, :] = h + 0.5 * _rms(acc_ref[rows, :], gpost_ref[...])


def _ffn(h, gpre, wi, wo, gpost, layer, which):
    rows, d = h.shape
    d_ff = wo.shape[2]
    return pl.pallas_call(
        functools.partial(_ffn_body, d_ff=d_ff),
        out_shape=jax.ShapeDtypeStruct((rows, d), F32),
        grid=(rows // FFN_ROW_TILE,),
        in_specs=[
            pl.BlockSpec((FFN_ROW_TILE, d), lambda i: (i, 0)),
            _const_spec((None, None, 1, d), (layer, 3 * which, 0, 0)),
            _const_spec((None, None, d, 2 * d_ff), (layer, which, 0, 0)),
            _const_spec((None, None, d_ff, d), (layer, which, 0, 0)),
            _const_spec((None, None, 1, d), (layer, 3 * which, 0, 0)),
        ],
        out_specs=pl.BlockSpec((FFN_ROW_TILE, d), lambda i: (i, 0)),
        scratch_shapes=[pltpu.VMEM((FFN_ROW_TILE, d), F32)],
        compiler_params=_params(1),
        name="ffn",
    )(h, gpre, wi, wo, gpost)


def _inproj_body(h_ref, gpre_ref, w_ref, wdt_ref, lng_ref, sw_ref, sb_ref, cw_ref, cb_ref, dtb_ref,
                 a_ref, zxd_ref, bc_ref, *rest, tiles_per_seq):
    kvq_refs, (slab_ref, ext_ref) = rest[:-2], rest[-2:]

    first = pl.program_id(0) % tiles_per_seq == 0

    @pl.when(first)
    def _():
        ext_ref[0:CONV_PAD, :] = jnp.zeros((CONV_PAD, SSM_CONV_CH), F32)

    @pl.when(jnp.logical_not(first))
    def _():
        ext_ref[0:CONV_PAD, :] = ext_ref[ROW_TILE:ROW_TILE + CONV_PAD, :]

    hn = _rms(h_ref[...], gpre_ref[...]).astype(BF16)

    def proj(lo, hi):
        return _dot(hn, w_ref[:, lo:hi])

    def stage_xbc(g):
        lo = g * MXU_N
        ext_ref[CONV_PAD:CONV_PAD + ROW_TILE, lo:lo + MXU_N] = proj(OFF_XBC + lo, OFF_XBC + lo + MXU_N)

    def conv_chunk(c, lo, hi, out_ref, out_lo):
        r0 = c * CHUNK
        for l0 in range(lo, hi, LANES):
            ls = slice(l0, l0 + LANES)
            win = ext_ref[r0:r0 + CONV_PAD + CHUNK, ls]
            acc = cb_ref[:, ls] + win[CONV_PAD:] * cw_ref[SSM_CONV - 1:SSM_CONV, ls]
            for back in range(1, SSM_CONV):
                tap = SSM_CONV - 1 - back
                acc = acc + pltpu.roll(win, back, 0)[CONV_PAD:] * cw_ref[tap:tap + 1, ls]
            o0 = out_lo + l0 - lo
            out_ref[r0:r0 + CHUNK, o0:o0 + LANES] = _silu(acc).astype(out_ref.dtype)

    conv_x = lambda c: conv_chunk(c, 0, SSM_INNER, zxd_ref, SSM_INNER)
    conv_bc = lambda c: conv_chunk(c, SSM_INNER, SSM_CONV_CH, bc_ref, 0)

    def dt_half(g):
        ls = slice(g * MXU_N, (g + 1) * MXU_N)
        zxd_ref[:, 2 * SSM_INNER + g * MXU_N:2 * SSM_INNER + (g + 1) * MXU_N] = (
            _softplus(_dot(hn, wdt_ref[:, ls]) + dtb_ref[:, ls]))

    uv = proj(0, OFF_Z)
    stage_xbc(0)
    stage_xbc(1)

    uv = jax.nn.gelu(uv)
    u = uv[:, :SGU_WIDTH]
    v = uv[:, SGU_WIDTH:]
    mu = jnp.mean(v, axis=-1, keepdims=True)
    var = jnp.mean(jnp.square(v - mu), axis=-1, keepdims=True)
    vn = ((v - mu) * lax.rsqrt(var + EPS) * lng_ref[...]).astype(BF16)
    causal = (lax.broadcasted_iota(jnp.int32, (CHUNK, CHUNK), 0)
              >= lax.broadcasted_iota(jnp.int32, (CHUNK, CHUNK), 1))
    lane_group = lax.broadcasted_iota(jnp.int32, (1, SGU_WIDTH), 1) // HEAD_DIM
    w_cat = jnp.concatenate([jnp.where(causal, sw_ref[g], 0.0).astype(BF16) for g in range(SGU_GROUPS)], axis=1)
    for c in range(ROW_TILE // CHUNK):
        r0 = c * CHUNK
        vc = vn[r0:r0 + CHUNK]
        v_blocks = [jnp.where(lane_group == g, vc, jnp.zeros_like(vc)) for g in range(SGU_GROUPS)]
        s = sb_ref[...] + _dot(w_cat, jnp.concatenate(v_blocks, axis=0))
        a_ref[r0:r0 + CHUNK, :] = (u[r0:r0 + CHUNK] * s).astype(BF16)

    stage_xbc(2)
    conv_x(0)
    stage_xbc(3)
    conv_x(1)

    n_slab = ATTN_WIDTH // LANES
    for t, vector_step in enumerate((lambda: conv_x(2), lambda: conv_x(3), lambda: conv_bc(0))):
        val = proj(OFF_DT + t * ATTN_WIDTH, OFF_DT + (t + 1) * ATTN_WIDTH)
        if t == 0:
            val = val * (HEAD_DIM ** -0.5 * LOG2E)
        for sl in range(n_slab):
            slab_ref[t * n_slab + sl] = val[:, sl * LANES:(sl + 1) * LANES]
        plane = (t + 2) % 3
        for bi, (_, dil) in enumerate(DILATED_PATTERNS):
            out = kvq_refs[bi]
            if dil == 1:
                out[plane] = val.astype(BF16)
                continue
            for r in range(dil):
                for sl in range(n_slab):
                    lo = r * ATTN_WIDTH + sl * LANES
                    out[plane, :, lo:lo + LANES] = (
                        slab_ref[t * n_slab + sl, pl.ds(r, ROW_TILE // dil, stride=dil), :].astype(BF16))
        vector_step()

    dt_half(0)
    conv_bc(1)
    dt_half(1)
    conv_bc(2)
    zxd_ref[:, 0:MXU_N] = proj(OFF_Z, OFF_Z + MXU_N)
    conv_bc(3)
    zxd_ref[:, MXU_N:SSM_INNER] = proj(OFF_Z + MXU_N, OFF_XBC)


def _inproj(h, gpre, w_main, w_dt, lng, sgu_w, sgu_b, conv_w, conv_b, dtb, layer, tiles_per_seq):
    rows, d = h.shape
    row_spec = lambda w: pl.BlockSpec((ROW_TILE, w), lambda i: (i, 0))
    vec = lambda w: _const_spec((None, 1, w), (layer, 0, 0))
    outs = ((SGU_WIDTH, BF16), (3 * SSM_INNER, F32), (SSM_CONV_CH - SSM_INNER, BF16))
    out_shape = [jax.ShapeDtypeStruct((rows, w), t) for w, t in outs]
    out_specs = [row_spec(w) for w, _ in outs]
    for _, dil in DILATED_PATTERNS:
        out_shape.append(jax.ShapeDtypeStruct((3, rows // dil, ATTN_WIDTH * dil), BF16))
        out_specs.append(pl.BlockSpec((3, ROW_TILE // dil, ATTN_WIDTH * dil), lambda i: (0, i, 0)))
    return pl.pallas_call(
        functools.partial(_inproj_body, tiles_per_seq=tiles_per_seq),
        out_shape=out_shape,
        grid=(rows // ROW_TILE,),
        in_specs=[
            row_spec(d),
            _const_spec((None, None, 1, d), (layer, 1, 0, 0)),
            _const_spec((None, d, W_MAIN), (layer, 0, 0)),
            _const_spec((None, d, SSM_INNER), (layer, 0, 0)),
            vec(SGU_WIDTH),
            _const_spec((None, SGU_GROUPS, CHUNK, CHUNK), (layer, 0, 0, 0)),
            _const_spec((None, CHUNK, SGU_WIDTH), (layer, 0, 0)),
            _const_spec((None, SSM_CONV, SSM_CONV_CH), (layer, 0, 0)),
            vec(SSM_CONV_CH), vec(SSM_INNER),
        ],
        out_specs=out_specs,
        scratch_shapes=[pltpu.VMEM((3 * ATTN_WIDTH // LANES, ROW_TILE, LANES), F32),
                        pltpu.VMEM((CONV_PAD + ROW_TILE, SSM_CONV_CH), F32)],
        compiler_params=_params(1),
        name="inproj_sgu",
    )(h, gpre, w_main, w_dt, lng, sgu_w, sgu_b, conv_w, conv_b, dtb)


def _softplus(x):
    return jnp.maximum(x, 0.0) + jnp.log1p(jnp.exp(-jnp.abs(x)))


def _cumsum_rows(tril, a):
    hi = a.astype(BF16)
    r1 = a - hi.astype(F32)
    mid = r1.astype(BF16)
    lo = (r1 - mid.astype(F32)).astype(BF16)
    return _dot(tril, hi) + _dot(tril, mid) + _dot(tril, lo)


def _ssd_body(zxd_ref, bc_ref, alog_ref, d_ref, ng_ref, o_ref, hs_ref):
    @pl.when(pl.program_id(1) == 0)
    def _():
        hs_ref[...] = jnp.zeros_like(hs_ref)

    row = lax.broadcasted_iota(jnp.int32, (CHUNK, CHUNK), 0)
    col = lax.broadcasted_iota(jnp.int32, (CHUNK, CHUNK), 1)
    causal = row >= col
    tril = jnp.where(causal, 1.0, 0.0).astype(BF16)
    low_half = col < HEAD_DIM
    lane_head = lax.broadcasted_iota(jnp.int32, (1, GROUP_W), 1) // HEAD_DIM
    a_neg = -jnp.exp(alog_ref[...]) * LOG2E
    c_off = SSM_GROUPS * SSM_STATE

    for c in range(SSD_TILE // CHUNK):
        r0 = c * CHUNK
        xs = zxd_ref[0, r0:r0 + CHUNK, SSM_INNER:2 * SSM_INNER]
        dt = zxd_ref[0, r0:r0 + CHUNK, 2 * SSM_INNER:3 * SSM_INNER]
        acs = _cumsum_rows(tril, dt * a_neg)
        acs_last = acs[CHUNK - 1:CHUNK, :]
        eacs = jnp.exp2(acs)
        x = xs * dt
        xb = x.astype(BF16)
        xd = (x * jnp.exp2(acs_last - acs)).astype(BF16)
        for g in range(SSM_GROUPS):
            gl = slice(g * GROUP_W, (g + 1) * GROUP_W)
            bg = bc_ref[0, r0:r0 + CHUNK, g * SSM_STATE:(g + 1) * SSM_STATE]
            cg = bc_ref[0, r0:r0 + CHUNK, c_off + g * SSM_STATE:c_off + (g + 1) * SSM_STATE]
            cbm = _dot_nt(cg, bg)
            xg = xb[:, gl]
            weights = []
            for e in range(SSM_HEADS // SSM_GROUPS):
                head = g * (SSM_HEADS // SSM_GROUPS) + e
                pair = acs[:, (head // 2) * 2 * HEAD_DIM:(head // 2 + 1) * 2 * HEAD_DIM]
                swapped = pltpu.roll(pair, HEAD_DIM, 1)
                acs_col = jnp.where(low_half, pair, swapped) if head % 2 == 0 else jnp.where(low_half, swapped, pair)
                seg = acs_col - acs_col.T
                decay = jnp.exp2(jnp.where(causal, seg, -jnp.inf))
                weights.append((cbm * decay).astype(BF16))
            x_blocks = [jnp.where(lane_head == e, xg, jnp.zeros_like(xg)) for e in range(SSM_HEADS // SSM_GROUPS)]
            yg = _dot(jnp.concatenate(weights, axis=1), jnp.concatenate(x_blocks, axis=0))
            h_prev = hs_ref[g]
            y_off = _dot(cg, h_prev.astype(BF16)) * eacs[:, gl]
            hs_ref[g] = h_prev * eacs[CHUNK - 1:CHUNK, gl] + _dot(bg.astype(F32).T.astype(BF16), xd[:, gl])
            y = yg + y_off + d_ref[:, gl] * xs[:, gl]
            y = y * _silu(zxd_ref[0, r0:r0 + CHUNK, gl])
            o_ref[0, r0:r0 + CHUNK, gl] = _rms(y, ng_ref[:, gl]).astype(BF16)


def _ssd(zxd, bc, alog, dskip, norm_g, layer):
    b, s, _ = zxd.shape
    tile = lambda w: pl.BlockSpec((1, SSD_TILE, w), lambda i, j: (i, j, 0))
    vec = lambda w: _const_spec((None, 1, w), (layer, 0, 0))
    return pl.pallas_call(
        _ssd_body,
        out_shape=jax.ShapeDtypeStruct((b, s, SSM_INNER), BF16),
        grid=(b, s // SSD_TILE),
        in_specs=[tile(3 * SSM_INNER), tile(SSM_CONV_CH - SSM_INNER), vec(SSM_INNER), vec(SSM_INNER), vec(SSM_INNER)],
        out_specs=tile(SSM_INNER),
        scratch_shapes=[pltpu.VMEM((SSM_GROUPS, SSM_STATE, GROUP_W), F32)],
        compiler_params=_params(2),
        name="ssd",
    )(zxd, bc, alog, dskip, norm_g)


def _attn_body(q_ref, kvp_ref, kvc_ref, bias_ref, ol_ref, *, tq, ncls):
    n = pl.program_id(2)
    nblk = tq // CHUNK
    w = ATTN_WIDTH
    lane_head = lax.broadcasted_iota(jnp.int32, (1, w), 1) // HEAD_DIM
    first_table = jnp.where(n == 0, 1, 0)
    for c in range(ncls):
        cl = slice(c * w, (c + 1) * w)
        for j in range(nblk):
            rows = slice(j * CHUNK, (j + 1) * CHUNK)
            q = q_ref[0, rows, cl]
            lhs = jnp.concatenate([jnp.where(lane_head == h, q, jnp.zeros_like(q)) for h in range(ATTN_HEADS)], axis=0)
            if j == 0:
                kk, vv = (jnp.concatenate([kvp_ref[t, 0, :, cl], kvc_ref[t, 0, 0:CHUNK, cl]], axis=0) for t in (0, 1))
            else:
                kk, vv = (kvc_ref[t, 0, (j - 1) * CHUNK:(j + 1) * CHUNK, cl] for t in (0, 1))
            logits = _dot_nt(lhs, kk) + bias_ref[first_table if j == 0 else 0]
            m = jnp.max(logits, axis=-1, keepdims=True)
            p = jnp.exp2(logits - m)
            s = jnp.sum(p, axis=-1, keepdims=True)
            pv = _dot(p.astype(BF16), vv) * (1.0 / s)
            lse = m + jnp.log2(s)
            o_acc = pv[0:CHUNK]
            lse_acc = jnp.broadcast_to(lse[0:CHUNK], (CHUNK, w))
            for h in range(1, ATTN_HEADS):
                hr = slice(h * CHUNK, (h + 1) * CHUNK)
                o_acc = jnp.where(lane_head == h, pv[hr], o_acc)
                lse_acc = jnp.where(lane_head == h, lse[hr], lse_acc)
            ol_ref[0, 0, rows, cl] = o_acc
            ol_ref[1, 0, rows, cl] = lse_acc


def _attn_branch(kvq, biases, branch):
    dil = DILATED_PATTERNS[branch][1]
    w = ATTN_WIDTH
    _, b, cls_len, _ = kvq.shape
    tq = min(ATTN_TILE, cls_len)
    ncls = min(dil, ATTN_TILE // tq)
    prev_block = lambda n: jnp.maximum(n * (tq // CHUNK) - 1, 0)
    return pl.pallas_call(
        functools.partial(_attn_body, tq=tq, ncls=ncls),
        out_shape=jax.ShapeDtypeStruct((2, b, cls_len, dil * w), F32),
        grid=(b, dil // ncls, cls_len // tq),
        in_specs=[pl.BlockSpec((None, 1, tq, ncls * w), lambda i, r, n: (2, i, n, r)),
                  pl.BlockSpec((2, 1, CHUNK, ncls * w), lambda i, r, n: (0, i, prev_block(n), r)),
                  pl.BlockSpec((2, 1, tq, ncls * w), lambda i, r, n: (0, i, n, r)),
                  _const_spec((None, 2, ATTN_HEADS * CHUNK, 2 * CHUNK), (branch, 0, 0, 0))],
        out_specs=pl.BlockSpec((2, 1, tq, ncls * w), lambda i, r, n: (0, i, n, r)),
        compiler_params=_params(3),
        name=f"dilated_attn_{dil}",
    )(kvq, kvq, kvq, biases)


def _t5_bucket(dist):
    max_exact = REL_BUCKETS // 2
    d = jnp.maximum(dist, 1).astype(F32)
    large = max_exact + (jnp.log(d / max_exact) / math.log(REL_MAX_DIST / max_exact)
                         * (REL_BUCKETS - max_exact)).astype(jnp.int32)
    large = jnp.minimum(large, REL_BUCKETS - 1)
    return jnp.where(dist < max_exact, dist, large)


def _bias_body(rel_ref, span_ref, bucket_ref, o_ref):
    bucket = bucket_ref[0]
    row = lax.broadcasted_iota(jnp.int32, (CHUNK, 2 * CHUNK), 0)
    col = lax.broadcasted_iota(jnp.int32, (CHUNK, 2 * CHUNK), 1)
    dist = row + CHUNK - col
    band = (dist >= 0) & (dist <= span_ref[pl.program_id(0)])
    band_cur = band & (col >= CHUNK)
    for h in range(ATTN_HEADS):
        bias = jnp.zeros((CHUNK, 2 * CHUNK), F32)
        for b in range(REL_BUCKETS):
            bias = jnp.where(bucket == b, rel_ref[b, h], bias)
        bias = bias * LOG2E
        o_ref[0, 0, h * CHUNK:(h + 1) * CHUNK, :] = jnp.where(band, bias, -jnp.inf)
        o_ref[0, 1, h * CHUNK:(h + 1) * CHUNK, :] = jnp.where(band_cur, bias, -jnp.inf)


def _bias_tables(rel_bias):
    assert all(window // dil <= CHUNK for window, dil in DILATED_PATTERNS)
    dist = jnp.arange(CHUNK)[:, None] + CHUNK - jnp.arange(2 * CHUNK)[None, :]
    buckets = jnp.stack([_t5_bucket(jnp.maximum(dist, 0) * dil) for _, dil in DILATED_PATTERNS]).astype(jnp.int32)
    spans = jnp.array([window // dil for window, dil in DILATED_PATTERNS], jnp.int32)
    nb = len(DILATED_PATTERNS)
    return pl.pallas_call(
        _bias_body,
        out_shape=jax.ShapeDtypeStruct((nb, 2, ATTN_HEADS * CHUNK, 2 * CHUNK), F32),
        grid=(nb,),
        in_specs=[pl.BlockSpec(memory_space=pltpu.SMEM), pl.BlockSpec(memory_space=pltpu.SMEM),
                  pl.BlockSpec((1, CHUNK, 2 * CHUNK), lambda i: (i, 0, 0))],
        out_specs=pl.BlockSpec((1, 2, ATTN_HEADS * CHUNK, 2 * CHUNK), lambda i: (i, 0, 0, 0)),
        compiler_params=_params(1),
        name="rel_bias_tables",
    )(rel_bias, spans, buckets)


def _mix_xattn_body(h_ref, a_ref, b_ref, *rest):
    nb = len(DILATED_PATTERNS)
    branch_refs = rest[:nb]
    w_ref, gpost_ref, xpre_ref, wq_ref, kv_ref, wo_ref, xpost_ref, out_ref, nat_ref = rest[nb:]
    n_slab = ATTN_WIDTH // LANES
    off_b = SGU_WIDTH
    off_c = SGU_WIDTH + SSM_INNER
    d = h_ref.shape[-1]
    hd = d // XATTN_HEADS

    def token_order(bi, plane):
        ref, dil = branch_refs[bi], DILATED_PATTERNS[bi][1]
        if dil == 1:
            return ref[plane]
        slot = (plane * nb + bi) * n_slab
        for r in range(dil):
            for sl in range(n_slab):
                lo = r * ATTN_WIDTH + sl * LANES
                nat_ref[slot + sl, pl.ds(r, ROW_TILE // dil, stride=dil), :] = ref[plane, :, lo:lo + LANES]
        return jnp.concatenate([nat_ref[slot + sl] for sl in range(n_slab)], axis=-1)

    o1, o2, o3 = (token_order(bi, 0) for bi in range(nb))
    l1, l2, l3 = (token_order(bi, 1) for bi in range(nb))
    m = jnp.maximum(jnp.maximum(l1, l2), l3)
    e1, e2, e3 = jnp.exp2(l1 - m), jnp.exp2(l2 - m), jnp.exp2(l3 - m)
    c = (e1 * o1 + e2 * o2 + e3 * o3) * (1.0 / (e1 + e2 + e3))
    y = (_dot(a_ref[...], w_ref[0:off_b, :])
         + _dot(b_ref[...], w_ref[off_b:off_c, :])
         + _dot(c.astype(BF16), w_ref[off_c:off_c + ATTN_WIDTH, :]))
    h = h_ref[...] + _rms(y, gpost_ref[...])

    hn = _rms(h, xpre_ref[...]).astype(BF16)
    q = (_dot(hn, wq_ref[...]) * (hd ** -0.5 * LOG2E)).astype(BF16)
    heads = []
    for i in range(XATTN_HEADS):
        hs = slice(i * hd, (i + 1) * hd)
        logits = _dot_nt(q[:, hs], kv_ref[0, :, hs])
        m = jnp.max(logits, axis=-1, keepdims=True)
        p = jnp.exp2(logits - m)
        s = jnp.sum(p, axis=-1, keepdims=True)
        heads.append((_dot(p.astype(BF16), kv_ref[0, :, d + i * hd:d + (i + 1) * hd]) * (1.0 / s)).astype(BF16))
    y = _dot(jnp.concatenate(heads, axis=-1), wo_ref[...])
    out_ref[...] = h + _rms(y, xpost_ref[...])


def _mix_xattn(h, a, bo, branch_outs, w_out, npost, npre, wq, kv, wo, layer, tiles_per_seq):
    rows, d = h.shape
    m = kv.shape[2]
    row_spec = lambda w: pl.BlockSpec((ROW_TILE, w), lambda i: (i, 0))
    branch = [pl.BlockSpec((2, ROW_TILE // dil, ATTN_WIDTH * dil), lambda i: (0, i, 0)) for _, dil in DILATED_PATTERNS]
    return pl.pallas_call(
        _mix_xattn_body,
        out_shape=jax.ShapeDtypeStruct((rows, d), F32),
        grid=(rows // ROW_TILE,),
        in_specs=[row_spec(d), row_spec(SGU_WIDTH), row_spec(SSM_INNER), *branch,
                  _const_spec((None, d, d), (layer, 0, 0)),
                  _const_spec((None, None, 1, d), (layer, 1, 0, 0)),
                  _const_spec((None, None, 1, d), (layer, 2, 0, 0)),
                  _const_spec((None, d, d), (layer, 0, 0)),
                  pl.BlockSpec((None, 1, m, 2 * d), lambda i: (layer, i // tiles_per_seq, 0, 0)),
                  _const_spec((None, d, d), (layer, 0, 0)),
                  _const_spec((None, None, 1, d), (layer, 2, 0, 0))],
        out_specs=row_spec(d),
        scratch_shapes=[pltpu.VMEM((2 * len(DILATED_PATTERNS) * ATTN_WIDTH // LANES, ROW_TILE, LANES), F32)],
        compiler_params=_params(1),
        name="mix_out_xattn",
    )(h, a, bo, *branch_outs, w_out, npost, npre, wq, kv, wo, npost)


def _memkv_body(mem_ref, g_ref, w_ref, kv_ref):
    mn = _rms(mem_ref[0], g_ref[...]).astype(BF16)
    kv_ref[0] = _dot(mn, w_ref[...]).astype(BF16)


def _memkv(mem, g, wkv):
    b, m, d = mem.shape
    depth = wkv.shape[0]
    return pl.pallas_call(
        _memkv_body,
        out_shape=jax.ShapeDtypeStruct((depth, b, m, 2 * d), BF16),
        grid=(depth, b),
        in_specs=[pl.BlockSpec((1, m, d), lambda l, i: (i, 0, 0)),
                  pl.BlockSpec((None, 1, d), lambda l, i: (l, 0, 0)),
                  pl.BlockSpec((None, d, 2 * d), lambda l, i: (l, 0, 0))],
        out_specs=pl.BlockSpec((None, 1, m, 2 * d), lambda l, i: (l, i, 0, 0)),
        compiler_params=_params(2),
        name="mem_kv",
    )(mem, g, wkv)


def kernel(x, mem, norm_pre, norm_post, ffn_wi, ffn_wo, mix_w_in, mix_w_out, sgu_ln_g, sgu_w, sgu_b, ssm_conv_w, ssm_conv_b, ssm_dt_bias, ssm_a_log, ssm_d, ssm_norm_g, rel_bias, mem_norm_g, xattn_wq, xattn_wkv, xattn_wo):
    b, s, d = x.shape
    depth = norm_pre.shape[0]
    assert s % ATTN_TILE == 0 and s % ROW_TILE == 0 and s % SSD_TILE == 0
    assert all(s % (dil * CHUNK) == 0 for _, dil in DILATED_PATTERNS)

    npre = norm_pre[:, :, None, :]
    npost = norm_post[:, :, None, :]
    wi = ffn_wi.astype(BF16)
    wo = ffn_wo.astype(BF16)
    w_main = jnp.concatenate([mix_w_in[..., :OFF_DT], mix_w_in[..., OFF_QKV:]], axis=-1).astype(BF16)
    w_dt = jnp.repeat(mix_w_in[..., OFF_DT:OFF_QKV], HEAD_DIM, axis=-1).astype(BF16)
    w_out = mix_w_out.astype(BF16)
    lng = sgu_ln_g[:, None, :]
    sgu_b_e = jnp.repeat(jnp.swapaxes(sgu_b, 1, 2), HEAD_DIM, axis=-1)
    per_head = lambda p: jnp.repeat(p, HEAD_DIM, axis=-1)[:, None, :]
    dtb, alog, dskip = per_head(ssm_dt_bias), per_head(ssm_a_log), per_head(ssm_d)
    conv_b = ssm_conv_b[:, None, :]
    norm_g = ssm_norm_g[:, None, :]
    mem_g = mem_norm_g[:, None, :]
    wq, wkv, wxo = xattn_wq.astype(BF16), xattn_wkv.astype(BF16), xattn_wo.astype(BF16)
    biases = _bias_tables(rel_bias)
    mkv = _memkv(mem, mem_g, wkv)

    h = x.reshape(b * s, d)
    for l in range(depth):
        h = _ffn(h, npre, wi, wo, npost, l, 0)
        a, zxd, bc, *kvq = _inproj(h, npre, w_main, w_dt, lng, sgu_w, sgu_b_e, ssm_conv_w, conv_b, dtb, l,
                                   s // ROW_TILE)
        seq = lambda t: t.reshape(b, s, t.shape[-1])
        bo = _ssd(seq(zxd), seq(bc), alog, dskip, norm_g, l)
        branch_outs = []
        for branch, t in enumerate(kvq):
            ol = _attn_branch(t.reshape(3, b, -1, t.shape[-1]), biases, branch)
            branch_outs.append(ol.reshape(2, -1, ol.shape[-1]))
        h = _mix_xattn(h, a, bo.reshape(b * s, SSM_INNER), branch_outs, w_out, npost, npre, wq, mkv, wxo, l,
                       s // ROW_TILE)
        h = _ffn(h, npre, wi, wo, npost, l, 1)
    return h.reshape(b, s, d)
```

```python
import functools
import math

import jax
import jax.numpy as jnp
from jax import lax
from jax.experimental import pallas as pl
from jax.experimental.pallas import tpu as pltpu

F32 = jnp.float32
BF16 = jnp.bfloat16
EPS = 1e-6
LOG2E = math.log2(math.e)

LANES = 128
MXU_N = 256
HEAD_DIM = 64
CHUNK = 128
SGU_WIDTH = 256
SGU_GROUPS = SGU_WIDTH // HEAD_DIM
SSM_INNER = 512
SSM_HEADS = SSM_INNER // HEAD_DIM
SSM_GROUPS = 2
SSM_STATE = 128
SSM_CONV = 4
SSM_CONV_CH = SSM_INNER + 2 * SSM_GROUPS * SSM_STATE
GROUP_W = SSM_INNER // SSM_GROUPS
ATTN_WIDTH = 256
ATTN_HEADS = ATTN_WIDTH // HEAD_DIM
DILATED_PATTERNS = ((128, 1), (512, 4), (2048, 16))
REL_BUCKETS = 32
REL_MAX_DIST = 2048
XATTN_HEADS = 4

OFF_Z = 2 * SGU_WIDTH
OFF_XBC = OFF_Z + SSM_INNER
OFF_DT = OFF_XBC + SSM_CONV_CH
OFF_QKV = OFF_DT + SSM_HEADS
W_MAIN = OFF_DT + 3 * ATTN_WIDTH

ROW_TILE = 512
FFN_ROW_TILE = 1024
FF_TILE = 256
SSD_TILE = 1024
ATTN_TILE = 4096
CONV_PAD = 8
VMEM_LIMIT = 56 * 1024 * 1024
SINGLE_BUFFER_MIN_ELEMS = 1 << 19


def _params(n_axes):
    return pltpu.CompilerParams(dimension_semantics=("arbitrary",) * n_axes,
                                vmem_limit_bytes=VMEM_LIMIT)


def _rms(x, g):
    return x * lax.rsqrt(jnp.mean(x * x, axis=-1, keepdims=True) + EPS) * g


def _silu(x):
    return x * jax.nn.sigmoid(x)


def _dot(a, b):
    return jnp.dot(a, b, preferred_element_type=F32)


def _dot_nt(a, b):
    return lax.dot_general(a, b, (((1,), (1,)), ((), ())), preferred_element_type=F32)


def _const_spec(shape, index):
    n_elems = math.prod(dim for dim in shape if dim is not None)
    mode = pl.Buffered(1) if n_elems >= SINGLE_BUFFER_MIN_ELEMS else None
    return pl.BlockSpec(shape, lambda *_: index, pipeline_mode=mode)


def _ffn_body(h_ref, gpre_ref, wi_ref, wo_ref, gpost_ref, o_ref, acc_ref, *, d_ff):
    for r0 in range(0, FFN_ROW_TILE, ROW_TILE):
        rows = slice(r0, r0 + ROW_TILE)
        h = h_ref[rows, :]
        hn = _rms(h, gpre_ref[...]).astype(BF16)
        for c in range(d_ff // FF_TILE):
            lo = c * FF_TILE
            g = _dot(hn, wi_ref[:, lo:lo + FF_TILE])
            u = _dot(hn, wi_ref[:, d_ff + lo:d_ff + lo + FF_TILE])
            y = _dot((_silu(g) * u).astype(BF16), wo_ref[lo:lo + FF_TILE, :])
            if c == 0:
                acc_ref[rows, :] = y
            else:
                acc_ref[rows, :] += y
        o_ref[rows, :] = h + 0.5 * _rms(acc_ref[rows, :], gpost_ref[...])


def _ffn(h, gpre, wi, wo, gpost, layer, which):
    rows, d = h.shape
    d_ff = wo.shape[2]
    return pl.pallas_call(
        functools.partial(_ffn_body, d_ff=d_ff),
        out_shape=jax.ShapeDtypeStruct((rows, d), F32),
        grid=(rows // FFN_ROW_TILE,),
        in_specs=[
            pl.BlockSpec((FFN_ROW_TILE, d), lambda i: (i, 0)),
            _const_spec((None, None, 1, d), (layer, 3 * which, 0, 0)),
            _const_spec((None, None, d, 2 * d_ff), (layer, which, 0, 0)),
            _const_spec((None, None, d_ff, d), (layer, which, 0, 0)),
            _const_spec((None, None, 1, d), (layer, 3 * which, 0, 0)),
        ],
        out_specs=pl.BlockSpec((FFN_ROW_TILE, d), lambda i: (i, 0)),
        scratch_shapes=[pltpu.VMEM((FFN_ROW_TILE, d), F32)],
        compiler_params=_params(1),
        name="ffn",
    )(h, gpre, wi, wo, gpost)


def _inproj_body(h_ref, gpre_ref, w_ref, wdt_ref, lng_ref, sw_ref, sb_ref, cw_ref, cb_ref, dtb_ref,
                 alog_ref, dskip_ref, ng_ref, a_ref, bo_ref, *rest, tiles_per_seq):
    kvq_refs, (slab_ref, ext_ref, zxd_ref, bc_ref, hs_ref) = rest[:-5], rest[-5:]

    first = pl.program_id(0) % tiles_per_seq == 0

    @pl.when(first)
    def _():
        ext_ref[0:CONV_PAD, :] = jnp.zeros((CONV_PAD, SSM_CONV_CH), F32)
        hs_ref[...] = jnp.zeros_like(hs_ref)

    @pl.when(jnp.logical_not(first))
    def _():
        ext_ref[0:CONV_PAD, :] = ext_ref[ROW_TILE:ROW_TILE + CONV_PAD, :]

    hn = _rms(h_ref[...], gpre_ref[...]).astype(BF16)

    def proj(lo, hi):
        return _dot(hn, w_ref[:, lo:hi])

    def stage_xbc(g):
        lo = g * MXU_N
        ext_ref[CONV_PAD:CONV_PAD + ROW_TILE, lo:lo + MXU_N] = proj(OFF_XBC + lo, OFF_XBC + lo + MXU_N)

    def conv_chunk(c, lo, hi, out_ref, out_lo):
        r0 = c * CHUNK
        for l0 in range(lo, hi, LANES):
            ls = slice(l0, l0 + LANES)
            win = ext_ref[r0:r0 + CONV_PAD + CHUNK, ls]
            acc = cb_ref[:, ls] + win[CONV_PAD:] * cw_ref[SSM_CONV - 1:SSM_CONV, ls]
            for back in range(1, SSM_CONV):
                tap = SSM_CONV - 1 - back
                acc = acc + pltpu.roll(win, back, 0)[CONV_PAD:] * cw_ref[tap:tap + 1, ls]
            o0 = out_lo + l0 - lo
            out_ref[r0:r0 + CHUNK, o0:o0 + LANES] = _silu(acc).astype(out_ref.dtype)

    conv_x = lambda c: conv_chunk(c, 0, SSM_INNER, zxd_ref, SSM_INNER)
    conv_bc = lambda c: conv_chunk(c, SSM_INNER, SSM_CONV_CH, bc_ref, 0)

    def dt_half(g):
        ls = slice(g * MXU_N, (g + 1) * MXU_N)
        zxd_ref[:, 2 * SSM_INNER + g * MXU_N:2 * SSM_INNER + (g + 1) * MXU_N] = (
            _softplus(_dot(hn, wdt_ref[:, ls]) + dtb_ref[:, ls]))

    uv = proj(0, OFF_Z)
    stage_xbc(0)
    stage_xbc(1)

    uv = jax.nn.gelu(uv)
    u = uv[:, :SGU_WIDTH]
    v = uv[:, SGU_WIDTH:]
    mu = jnp.mean(v, axis=-1, keepdims=True)
    var = jnp.mean(jnp.square(v - mu), axis=-1, keepdims=True)
    vn = ((v - mu) * lax.rsqrt(var + EPS) * lng_ref[...]).astype(BF16)
    causal = (lax.broadcasted_iota(jnp.int32, (CHUNK, CHUNK), 0)
              >= lax.broadcasted_iota(jnp.int32, (CHUNK, CHUNK), 1))
    lane_group = lax.broadcasted_iota(jnp.int32, (1, SGU_WIDTH), 1) // HEAD_DIM
    w_cat = jnp.concatenate([jnp.where(causal, sw_ref[g], 0.0).astype(BF16) for g in range(SGU_GROUPS)], axis=1)
    for c in range(ROW_TILE // CHUNK):
        r0 = c * CHUNK
        vc = vn[r0:r0 + CHUNK]
        v_blocks = [jnp.where(lane_group == g, vc, jnp.zeros_like(vc)) for g in range(SGU_GROUPS)]
        s = sb_ref[...] + _dot(w_cat, jnp.concatenate(v_blocks, axis=0))
        a_ref[r0:r0 + CHUNK, :] = (u[r0:r0 + CHUNK] * s).astype(BF16)

    stage_xbc(2)
    conv_x(0)
    stage_xbc(3)
    conv_x(1)

    n_slab = ATTN_WIDTH // LANES
    for t, vector_step in enumerate((lambda: conv_x(2), lambda: conv_x(3), lambda: conv_bc(0))):
        val = proj(OFF_DT + t * ATTN_WIDTH, OFF_DT + (t + 1) * ATTN_WIDTH)
        if t == 0:
            val = val * (HEAD_DIM ** -0.5 * LOG2E)
        for sl in range(n_slab):
            slab_ref[t * n_slab + sl] = val[:, sl * LANES:(sl + 1) * LANES]
        plane = (t + 2) % 3
        for bi, (_, dil) in enumerate(DILATED_PATTERNS):
            out = kvq_refs[bi]
            if dil == 1:
                out[plane] = val.astype(BF16)
                continue
            for r in range(dil):
                for sl in range(n_slab):
                    lo = r * ATTN_WIDTH + sl * LANES
                    out[plane, :, lo:lo + LANES] = (
                        slab_ref[t * n_slab + sl, pl.ds(r, ROW_TILE // dil, stride=dil), :].astype(BF16))
        vector_step()

    dt_half(0)
    conv_bc(1)
    dt_half(1)
    conv_bc(2)
    zxd_ref[:, 0:MXU_N] = proj(OFF_Z, OFF_Z + MXU_N)
    conv_bc(3)
    zxd_ref[:, MXU_N:SSM_INNER] = proj(OFF_Z + MXU_N, OFF_XBC)

    _ssd_chunks(zxd_ref, bc_ref, alog_ref, dskip_ref, ng_ref, bo_ref, hs_ref, ROW_TILE)


def _inproj(h, gpre, w_main, w_dt, lng, sgu_w, sgu_b, conv_w, conv_b, dtb, alog, dskip, norm_g, layer, tiles_per_seq):
    rows, d = h.shape
    row_spec = lambda w: pl.BlockSpec((ROW_TILE, w), lambda i: (i, 0))
    vec = lambda w: _const_spec((None, 1, w), (layer, 0, 0))
    outs = ((SGU_WIDTH, BF16), (SSM_INNER, BF16))
    out_shape = [jax.ShapeDtypeStruct((rows, w), t) for w, t in outs]
    out_specs = [row_spec(w) for w, _ in outs]
    for _, dil in DILATED_PATTERNS:
        out_shape.append(jax.ShapeDtypeStruct((3, rows // dil, ATTN_WIDTH * dil), BF16))
        out_specs.append(pl.BlockSpec((3, ROW_TILE // dil, ATTN_WIDTH * dil), lambda i: (0, i, 0)))
    return pl.pallas_call(
        functools.partial(_inproj_body, tiles_per_seq=tiles_per_seq),
        out_shape=out_shape,
        grid=(rows // ROW_TILE,),
        in_specs=[
            row_spec(d),
            _const_spec((None, None, 1, d), (layer, 1, 0, 0)),
            _const_spec((None, d, W_MAIN), (layer, 0, 0)),
            _const_spec((None, d, SSM_INNER), (layer, 0, 0)),
            vec(SGU_WIDTH),
            _const_spec((None, SGU_GROUPS, CHUNK, CHUNK), (layer, 0, 0, 0)),
            _const_spec((None, CHUNK, SGU_WIDTH), (layer, 0, 0)),
            _const_spec((None, SSM_CONV, SSM_CONV_CH), (layer, 0, 0)),
            vec(SSM_CONV_CH), vec(SSM_INNER), vec(SSM_INNER), vec(SSM_INNER), vec(SSM_INNER),
        ],
        out_specs=out_specs,
        scratch_shapes=[pltpu.VMEM((3 * ATTN_WIDTH // LANES, ROW_TILE, LANES), F32),
                        pltpu.VMEM((CONV_PAD + ROW_TILE, SSM_CONV_CH), F32),
                        pltpu.VMEM((ROW_TILE, 3 * SSM_INNER), F32),
                        pltpu.VMEM((ROW_TILE, SSM_CONV_CH - SSM_INNER), BF16),
                        pltpu.VMEM((SSM_GROUPS, SSM_STATE, GROUP_W), F32)],
        compiler_params=_params(1),
        name="inproj_sgu_ssd",
    )(h, gpre, w_main, w_dt, lng, sgu_w, sgu_b, conv_w, conv_b, dtb, alog, dskip, norm_g)


def _softplus(x):
    return jnp.maximum(x, 0.0) + jnp.log1p(jnp.exp(-jnp.abs(x)))


def _cumsum_rows(tril, a):
    hi = a.astype(BF16)
    r1 = a - hi.astype(F32)
    mid = r1.astype(BF16)
    lo = (r1 - mid.astype(F32)).astype(BF16)
    return _dot(tril, hi) + _dot(tril, mid) + _dot(tril, lo)


def _ssd_chunks(zxd_ref, bc_ref, alog_ref, d_ref, ng_ref, o_ref, hs_ref, n_tokens):
    row = lax.broadcasted_iota(jnp.int32, (CHUNK, CHUNK), 0)
    col = lax.broadcasted_iota(jnp.int32, (CHUNK, CHUNK), 1)
    causal = row >= col
    tril = jnp.where(causal, 1.0, 0.0).astype(BF16)
    low_half = col < HEAD_DIM
    lane_head = lax.broadcasted_iota(jnp.int32, (1, GROUP_W), 1) // HEAD_DIM
    a_neg = -jnp.exp(alog_ref[...]) * LOG2E
    c_off = SSM_GROUPS * SSM_STATE

    for c in range(n_tokens // CHUNK):
        r0 = c * CHUNK
        xs = zxd_ref[r0:r0 + CHUNK, SSM_INNER:2 * SSM_INNER]
        dt = zxd_ref[r0:r0 + CHUNK, 2 * SSM_INNER:3 * SSM_INNER]
        acs = _cumsum_rows(tril, dt * a_neg)
        acs_last = acs[CHUNK - 1:CHUNK, :]
        eacs = jnp.exp2(acs)
        x = xs * dt
        xb = x.astype(BF16)
        xd = (x * jnp.exp2(acs_last - acs)).astype(BF16)
        for g in range(SSM_GROUPS):
            gl = slice(g * GROUP_W, (g + 1) * GROUP_W)
            bg = bc_ref[r0:r0 + CHUNK, g * SSM_STATE:(g + 1) * SSM_STATE]
            cg = bc_ref[r0:r0 + CHUNK, c_off + g * SSM_STATE:c_off + (g + 1) * SSM_STATE]
            cbm = _dot_nt(cg, bg)
            xg = xb[:, gl]
            weights = []
            for e in range(SSM_HEADS // SSM_GROUPS):
                head = g * (SSM_HEADS // SSM_GROUPS) + e
                pair = acs[:, (head // 2) * 2 * HEAD_DIM:(head // 2 + 1) * 2 * HEAD_DIM]
                swapped = pltpu.roll(pair, HEAD_DIM, 1)
                acs_col = jnp.where(low_half, pair, swapped) if head % 2 == 0 else jnp.where(low_half, swapped, pair)
                seg = acs_col - acs_col.T
                decay = jnp.exp2(jnp.where(causal, seg, -jnp.inf))
                weights.append((cbm * decay).astype(BF16))
            x_blocks = [jnp.where(lane_head == e, xg, jnp.zeros_like(xg)) for e in range(SSM_HEADS // SSM_GROUPS)]
            yg = _dot(jnp.concatenate(weights, axis=1), jnp.concatenate(x_blocks, axis=0))
            h_prev = hs_ref[g]
            y_off = _dot(cg, h_prev.astype(BF16)) * eacs[:, gl]
            hs_ref[g] = h_prev * eacs[CHUNK - 1:CHUNK, gl] + _dot(bg.astype(F32).T.astype(BF16), xd[:, gl])
            y = yg + y_off + d_ref[:, gl] * xs[:, gl]
            y = y * _silu(zxd_ref[r0:r0 + CHUNK, gl])
            o_ref[r0:r0 + CHUNK, gl] = _rms(y, ng_ref[:, gl]).astype(BF16)


def _attn_body(q_ref, kvp_ref, kvc_ref, bias_ref, ol_ref, *, tq, ncls):
    n = pl.program_id(2)
    nblk = tq // CHUNK
    w = ATTN_WIDTH
    lane_head = lax.broadcasted_iota(jnp.int32, (1, w), 1) // HEAD_DIM
    first_table = jnp.where(n == 0, 1, 0)
    for c in range(ncls):
        cl = slice(c * w, (c + 1) * w)
        for j in range(nblk):
            rows = slice(j * CHUNK, (j + 1) * CHUNK)
            q = q_ref[0, rows, cl]
            lhs = jnp.concatenate([jnp.where(lane_head == h, q, jnp.zeros_like(q)) for h in range(ATTN_HEADS)], axis=0)
            if j == 0:
                kk, vv = (jnp.concatenate([kvp_ref[t, 0, :, cl], kvc_ref[t, 0, 0:CHUNK, cl]], axis=0) for t in (0, 1))
            else:
                kk, vv = (kvc_ref[t, 0, (j - 1) * CHUNK:(j + 1) * CHUNK, cl] for t in (0, 1))
            logits = _dot_nt(lhs, kk) + bias_ref[first_table if j == 0 else 0]
            m = jnp.max(logits, axis=-1, keepdims=True)
            p = jnp.exp2(logits - m)
            s = jnp.sum(p, axis=-1, keepdims=True)
            pv = _dot(p.astype(BF16), vv) * (1.0 / s)
            lse = m + jnp.log2(s)
            o_acc = pv[0:CHUNK]
            lse_acc = jnp.broadcast_to(lse[0:CHUNK], (CHUNK, w))
            for h in range(1, ATTN_HEADS):
                hr = slice(h * CHUNK, (h + 1) * CHUNK)
                o_acc = jnp.where(lane_head == h, pv[hr], o_acc)
                lse_acc = jnp.where(lane_head == h, lse[hr], lse_acc)
            ol_ref[0, 0, rows, cl] = o_acc
            ol_ref[1, 0, rows, cl] = lse_acc


def _attn_branch(kvq, biases, branch):
    dil = DILATED_PATTERNS[branch][1]
    w = ATTN_WIDTH
    _, b, cls_len, _ = kvq.shape
    tq = min(ATTN_TILE, cls_len)
    ncls = min(dil, ATTN_TILE // tq)
    prev_block = lambda n: jnp.maximum(n * (tq // CHUNK) - 1, 0)
    return pl.pallas_call(
        functools.partial(_attn_body, tq=tq, ncls=ncls),
        out_shape=jax.ShapeDtypeStruct((2, b, cls_len, dil * w), F32),
        grid=(b, dil // ncls, cls_len // tq),
        in_specs=[pl.BlockSpec((None, 1, tq, ncls * w), lambda i, r, n: (2, i, n, r)),
                  pl.BlockSpec((2, 1, CHUNK, ncls * w), lambda i, r, n: (0, i, prev_block(n), r)),
                  pl.BlockSpec((2, 1, tq, ncls * w), lambda i, r, n: (0, i, n, r)),
                  _const_spec((None, 2, ATTN_HEADS * CHUNK, 2 * CHUNK), (branch, 0, 0, 0))],
        out_specs=pl.BlockSpec((2, 1, tq, ncls * w), lambda i, r, n: (0, i, n, r)),
        compiler_params=_params(3),
        name=f"dilated_attn_{dil}",
    )(kvq, kvq, kvq, biases)


def _t5_bucket(dist):
    max_exact = REL_BUCKETS // 2
    d = jnp.maximum(dist, 1).astype(F32)
    large = max_exact + (jnp.log(d / max_exact) / math.log(REL_MAX_DIST / max_exact)
                         * (REL_BUCKETS - max_exact)).astype(jnp.int32)
    large = jnp.minimum(large, REL_BUCKETS - 1)
    return jnp.where(dist < max_exact, dist, large)


def _bias_body(rel_ref, span_ref, bucket_ref, o_ref):
    bucket = bucket_ref[0]
    row = lax.broadcasted_iota(jnp.int32, (CHUNK, 2 * CHUNK), 0)
    col = lax.broadcasted_iota(jnp.int32, (CHUNK, 2 * CHUNK), 1)
    dist = row + CHUNK - col
    band = (dist >= 0) & (dist <= span_ref[pl.program_id(0)])
    band_cur = band & (col >= CHUNK)
    for h in range(ATTN_HEADS):
        bias = jnp.zeros((CHUNK, 2 * CHUNK), F32)
        for b in range(REL_BUCKETS):
            bias = jnp.where(bucket == b, rel_ref[b, h], bias)
        bias = bias * LOG2E
        o_ref[0, 0, h * CHUNK:(h + 1) * CHUNK, :] = jnp.where(band, bias, -jnp.inf)
        o_ref[0, 1, h * CHUNK:(h + 1) * CHUNK, :] = jnp.where(band_cur, bias, -jnp.inf)


def _bias_tables(rel_bias):
    assert all(window // dil <= CHUNK for window, dil in DILATED_PATTERNS)
    dist = jnp.arange(CHUNK)[:, None] + CHUNK - jnp.arange(2 * CHUNK)[None, :]
    buckets = jnp.stack([_t5_bucket(jnp.maximum(dist, 0) * dil) for _, dil in DILATED_PATTERNS]).astype(jnp.int32)
    spans = jnp.array([window // dil for window, dil in DILATED_PATTERNS], jnp.int32)
    nb = len(DILATED_PATTERNS)
    return pl.pallas_call(
        _bias_body,
        out_shape=jax.ShapeDtypeStruct((nb, 2, ATTN_HEADS * CHUNK, 2 * CHUNK), F32),
        grid=(nb,),
        in_specs=[pl.BlockSpec(memory_space=pltpu.SMEM), pl.BlockSpec(memory_space=pltpu.SMEM),
                  pl.BlockSpec((1, CHUNK, 2 * CHUNK), lambda i: (i, 0, 0))],
        out_specs=pl.BlockSpec((1, 2, ATTN_HEADS * CHUNK, 2 * CHUNK), lambda i: (i, 0, 0, 0)),
        compiler_params=_params(1),
        name="rel_bias_tables",
    )(rel_bias, spans, buckets)


def _mix_xattn_body(h_ref, a_ref, b_ref, *rest):
    nb = len(DILATED_PATTERNS)
    branch_refs = rest[:nb]
    w_ref, gpost_ref, xpre_ref, wq_ref, kv_ref, wo_ref, xpost_ref, out_ref, nat_ref = rest[nb:]
    n_slab = ATTN_WIDTH // LANES
    off_b = SGU_WIDTH
    off_c = SGU_WIDTH + SSM_INNER
    d = h_ref.shape[-1]
    hd = d // XATTN_HEADS

    def token_order(bi, plane):
        ref, dil = branch_refs[bi], DILATED_PATTERNS[bi][1]
        if dil == 1:
            return ref[plane]
        slot = (plane * nb + bi) * n_slab
        for r in range(dil):
            for sl in range(n_slab):
                lo = r * ATTN_WIDTH + sl * LANES
                nat_ref[slot + sl, pl.ds(r, ROW_TILE // dil, stride=dil), :] = ref[plane, :, lo:lo + LANES]
        return jnp.concatenate([nat_ref[slot + sl] for sl in range(n_slab)], axis=-1)

    o1, o2, o3 = (token_order(bi, 0) for bi in range(nb))
    l1, l2, l3 = (token_order(bi, 1) for bi in range(nb))
    m = jnp.maximum(jnp.maximum(l1, l2), l3)
    e1, e2, e3 = jnp.exp2(l1 - m), jnp.exp2(l2 - m), jnp.exp2(l3 - m)
    c = (e1 * o1 + e2 * o2 + e3 * o3) * (1.0 / (e1 + e2 + e3))
    y = (_dot(a_ref[...], w_ref[0:off_b, :])
         + _dot(b_ref[...], w_ref[off_b:off_c, :])
         + _dot(c.astype(BF16), w_ref[off_c:off_c + ATTN_WIDTH, :]))
    h = h_ref[...] + _rms(y, gpost_ref[...])

    hn = _rms(h, xpre_ref[...]).astype(BF16)
    q = (_dot(hn, wq_ref[...]) * (hd ** -0.5 * LOG2E)).astype(BF16)
    heads = []
    for i in range(XATTN_HEADS):
        hs = slice(i * hd, (i + 1) * hd)
        logits = _dot_nt(q[:, hs], kv_ref[0, :, hs])
        m = jnp.max(logits, axis=-1, keepdims=True)
        p = jnp.exp2(logits - m)
        s = jnp.sum(p, axis=-1, keepdims=True)
        heads.append((_dot(p.astype(BF16), kv_ref[0, :, d + i * hd:d + (i + 1) * hd]) * (1.0 / s)).astype(BF16))
    y = _dot(jnp.concatenate(heads, axis=-1), wo_ref[...])
    out_ref[...] = h + _rms(y, xpost_ref[...])


def _mix_xattn(h, a, bo, branch_outs, w_out, npost, npre, wq, kv, wo, layer, tiles_per_seq):
    rows, d = h.shape
    m = kv.shape[2]
    row_spec = lambda w: pl.BlockSpec((ROW_TILE, w), lambda i: (i, 0))
    branch = [pl.BlockSpec((2, ROW_TILE // dil, ATTN_WIDTH * dil), lambda i: (0, i, 0)) for _, dil in DILATED_PATTERNS]
    return pl.pallas_call(
        _mix_xattn_body,
        out_shape=jax.ShapeDtypeStruct((rows, d), F32),
        grid=(rows // ROW_TILE,),
        in_specs=[row_spec(d), row_spec(SGU_WIDTH), row_spec(SSM_INNER), *branch,
                  _const_spec((None, d, d), (layer, 0, 0)),
                  _const_spec((None, None, 1, d), (layer, 1, 0, 0)),
                  _const_spec((None, None, 1, d), (layer, 2, 0, 0)),
                  _const_spec((None, d, d), (layer, 0, 0)),
                  pl.BlockSpec((None, 1, m, 2 * d), lambda i: (layer, i // tiles_per_seq, 0, 0)),
                  _const_spec((None, d, d), (layer, 0, 0)),
                  _const_spec((None, None, 1, d), (layer, 2, 0, 0))],
        out_specs=row_spec(d),
        scratch_shapes=[pltpu.VMEM((2 * len(DILATED_PATTERNS) * ATTN_WIDTH // LANES, ROW_TILE, LANES), F32)],
        compiler_params=_params(1),
        name="mix_out_xattn",
    )(h, a, bo, *branch_outs, w_out, npost, npre, wq, kv, wo, npost)


def _memkv_body(mem_ref, g_ref, w_ref, kv_ref):
    mn = _rms(mem_ref[0], g_ref[...]).astype(BF16)
    kv_ref[0] = _dot(mn, w_ref[...]).astype(BF16)


def _memkv(mem, g, wkv):
    b, m, d = mem.shape
    depth = wkv.shape[0]
    return pl.pallas_call(
        _memkv_body,
        out_shape=jax.ShapeDtypeStruct((depth, b, m, 2 * d), BF16),
        grid=(depth, b),
        in_specs=[pl.BlockSpec((1, m, d), lambda l, i: (i, 0, 0)),
                  pl.BlockSpec((None, 1, d), lambda l, i: (l, 0, 0)),
                  pl.BlockSpec((None, d, 2 * d), lambda l, i: (l, 0, 0))],
        out_specs=pl.BlockSpec((None, 1, m, 2 * d), lambda l, i: (l, i, 0, 0)),
        compiler_params=_params(2),
        name="mem_kv",
    )(mem, g, wkv)


def kernel(x, mem, norm_pre, norm_post, ffn_wi, ffn_wo, mix_w_in, mix_w_out, sgu_ln_g, sgu_w, sgu_b, ssm_conv_w, ssm_conv_b, ssm_dt_bias, ssm_a_log, ssm_d, ssm_norm_g, rel_bias, mem_norm_g, xattn_wq, xattn_wkv, xattn_wo):
    b, s, d = x.shape
    depth = norm_pre.shape[0]
    assert s % ATTN_TILE == 0 and s % ROW_TILE == 0 and s % SSD_TILE == 0
    assert all(s % (dil * CHUNK) == 0 for _, dil in DILATED_PATTERNS)

    npre = norm_pre[:, :, None, :]
    npost = norm_post[:, :, None, :]
    wi = ffn_wi.astype(BF16)
    wo = ffn_wo.astype(BF16)
    w_main = jnp.concatenate([mix_w_in[..., :OFF_DT], mix_w_in[..., OFF_QKV:]], axis=-1).astype(BF16)
    w_dt = jnp.repeat(mix_w_in[..., OFF_DT:OFF_QKV], HEAD_DIM, axis=-1).astype(BF16)
    w_out = mix_w_out.astype(BF16)
    lng = sgu_ln_g[:, None, :]
    sgu_b_e = jnp.repeat(jnp.swapaxes(sgu_b, 1, 2), HEAD_DIM, axis=-1)
    per_head = lambda p: jnp.repeat(p, HEAD_DIM, axis=-1)[:, None, :]
    dtb, alog, dskip = per_head(ssm_dt_bias), per_head(ssm_a_log), per_head(ssm_d)
    conv_b = ssm_conv_b[:, None, :]
    norm_g = ssm_norm_g[:, None, :]
    mem_g = mem_norm_g[:, None, :]
    wq, wkv, wxo = xattn_wq.astype(BF16), xattn_wkv.astype(BF16), xattn_wo.astype(BF16)
    biases = _bias_tables(rel_bias)
    mkv = _memkv(mem, mem_g, wkv)

    h = x.reshape(b * s, d)
    for l in range(depth):
        h = _ffn(h, npre, wi, wo, npost, l, 0)
        a, bo, *kvq = _inproj(h, npre, w_main, w_dt, lng, sgu_w, sgu_b_e, ssm_conv_w, conv_b, dtb, alog, dskip,
                              norm_g, l, s // ROW_TILE)
        branch_outs = []
        for branch, t in enumerate(kvq):
            ol = _attn_branch(t.reshape(3, b, -1, t.shape[-1]), biases, branch)
            branch_outs.append(ol.reshape(2, -1, ol.shape[-1]))
        h = _mix_xattn(h, a, bo.reshape(b * s, SSM_INNER), branch_outs, w_out, npost, npre, wq, mkv, wxo, l,
                       s // ROW_TILE)
        h = _ffn(h, npre, wi, wo, npost, l, 1)
    return h.reshape(b, s, d)
```
